```python
import math, functools
import jax, jax.numpy as jnp
from jax import lax
import numpy as np

D_MODEL = 1024
BATCH = 4
SEQ = 4096
DEPTH = 2
DEC_BATCH = 128
DEC_SEQ = 8
PAST_LEN = 8192
PAGE_SIZE = 128

H_A = 8
QK_NOPE = 64
QK_ROPE = 32
V_A = 64
Q_LORA = 384
KV_LORA = 256
ROPE_THETA = 10000.0
H_B = 4
D_B = 64
MOBA_BLOCK = 256
MOBA_TOPK = 3
MEM_LEN = 256
H_M = 4
D_M = 64
MIX = H_A * V_A + H_B * D_B + H_M * D_M
N_IN = Q_LORA + KV_LORA + QK_ROPE + 3 * H_B * D_B + H_M * D_M
D_FF = 2816
CONV_W = 3
Q_BLOCK = 128
ALPHA = (2 * DEPTH) ** 0.25
BETA = (8 * DEPTH) ** -0.25
NORM_EPS = 1e-5

kernel_name = "hybrid_mla_moba_mem_convffn_step"


def layer_norm(x, g, b):
    xf = x.astype(jnp.float32)
    mu = jnp.mean(xf, -1, keepdims=True)
    xc = xf - mu
    var = jnp.mean(xc * xc, -1, keepdims=True)
    return (xc * lax.rsqrt(var + NORM_EPS) * g + b).astype(x.dtype)


def rms_norm(x, g):
    xf = x.astype(jnp.float32)
    return (xf * lax.rsqrt(jnp.mean(xf * xf, -1, keepdims=True) + 1e-6) * g).astype(x.dtype)


def rope(x, pos):
    p = x.shape[-1]
    inv = ROPE_THETA ** (-jnp.arange(0, p, 2, dtype=jnp.float32) / p)
    ang = pos.astype(jnp.float32)[:, None] * inv
    shp = (ang.shape[0],) + (1,) * (x.ndim - 3) + (p // 2,)
    cos, sin = jnp.cos(ang).reshape(shp), jnp.sin(ang).reshape(shp)
    xf = x.astype(jnp.float32)
    x1, x2 = xf[..., : p // 2], xf[..., p // 2:]
    return jnp.concatenate([x1 * cos - x2 * sin, x2 * cos + x1 * sin], -1).astype(x.dtype)


def alibi_slopes(n):
    return 2.0 ** (-8.0 * jnp.arange(1, n + 1, dtype=jnp.float32) / n)


def mla_core(q_lat, q_pe, ckv, kpe, q_pos, k_pos):
    scale = (QK_NOPE + QK_ROPE) ** -0.5
    s = (jnp.einsum('bqhr,bkr->bhqk', q_lat, ckv).astype(jnp.float32)
         + jnp.einsum('bqhp,bkp->bhqk', q_pe, kpe).astype(jnp.float32)) * scale
    mask = k_pos[None, :] <= q_pos[:, None]
    p = jax.nn.softmax(jnp.where(mask, s, -jnp.inf), axis=-1).astype(ckv.dtype)
    return jnp.einsum('bhqk,bkr->bqhr', p, ckv)


def mla_prompt(q_lat, q_pe, ckv, kpe):
    b, s = q_lat.shape[:2]
    nqb = s // Q_BLOCK
    ql = q_lat.reshape(b, nqb, Q_BLOCK, H_A, KV_LORA).swapaxes(0, 1)
    qp = q_pe.reshape(b, nqb, Q_BLOCK, H_A, QK_ROPE).swapaxes(0, 1)
    k_pos = jnp.arange(s)

    def step(a):
        qlb, qpb, i = a
        return mla_core(qlb, qpb, ckv, kpe, i * Q_BLOCK + jnp.arange(Q_BLOCK), k_pos)

    o = lax.map(step, (ql, qp, jnp.arange(nqb)))
    return o.swapaxes(0, 1).reshape(b, s, H_A, KV_LORA)


def mla_sample(q_lat, q_pe, ckv_new, kpe_new, pool_ckv, pool_kpe, page_table, layer):
    t = q_lat.shape[1]
    past = page_table.shape[1] * pool_ckv.shape[2]
    q_pos = past + jnp.arange(t)
    k_pos = jnp.arange(past + t)

    def one(a):
        ql, qp, cn, kn, pt = a
        c = jnp.concatenate([pool_ckv[layer, pt].reshape(past, KV_LORA), cn], 0)
        k = jnp.concatenate([pool_kpe[layer, pt].reshape(past, QK_ROPE), kn], 0)
        return mla_core(ql[None], qp[None], c[None], k[None], q_pos, k_pos)[0]

    return lax.map(one, (q_lat, q_pe, ckv_new, kpe_new, page_table))


def moba_core(q, kb, vb, kmean, own_k, own_v, q_pos, own_start, n_past, slopes):
    nb = kb.shape[0]
    k_sel = min(MOBA_TOPK, nb)
    h, t = q.shape[1], q.shape[0]
    scale = D_B ** -0.5
    gate = jnp.einsum('thd,nhd->htn', q, kmean).astype(jnp.float32)
    gate = jnp.where(jnp.arange(nb) < n_past, gate, -jnp.inf)
    _, idx = lax.top_k(gate, k_sel)
    sel_ok = jnp.arange(k_sel) < n_past
    gather = jax.vmap(lambda xh, ih: xh[ih])
    ksel = gather(kb.transpose(2, 0, 1, 3), idx)
    vsel = gather(vb.transpose(2, 0, 1, 3), idx)
    key_pos = idx[..., None] * MOBA_BLOCK + jnp.arange(MOBA_BLOCK)
    dist_sel = (q_pos[None, :, None, None] - key_pos).astype(jnp.float32)
    s_sel = (jnp.einsum('thd,htkjd->htkj', q, ksel).astype(jnp.float32) * scale
             - slopes[:, None, None, None] * dist_sel)
    s_sel = jnp.where(sel_ok[None, None, :, None], s_sel, -jnp.inf)
    own_pos = own_start + jnp.arange(MOBA_BLOCK)
    dist_own = q_pos[:, None] - own_pos[None, :]
    s_own = (jnp.einsum('thd,jhd->htj', q, own_k).astype(jnp.float32) * scale
             - slopes[:, None, None] * dist_own.astype(jnp.float32)[None])
    s_own = jnp.where((dist_own >= 0)[None], s_own, -jnp.inf)
    s = jnp.concatenate([s_sel.reshape(h, t, k_sel * MOBA_BLOCK), s_own], -1)
    p = jax.nn.softmax(s, axis=-1).astype(q.dtype)
    p_sel = p[..., : k_sel * MOBA_BLOCK].reshape(h, t, k_sel, MOBA_BLOCK)
    p_own = p[..., k_sel * MOBA_BLOCK:]
    return (jnp.einsum('htkj,htkjd->thd', p_sel, vsel)
            + jnp.einsum('htj,jhd->thd', p_own, own_v))


def moba_prompt(q, k, v, slopes):
    b, s = q.shape[:2]
    nb = -(-s // MOBA_BLOCK)
    pad = nb * MOBA_BLOCK - s
    kp = jnp.pad(k, ((0, 0), (0, pad), (0, 0), (0, 0))).reshape(b, nb, MOBA_BLOCK, H_B, D_B)
    vp = jnp.pad(v, ((0, 0), (0, pad), (0, 0), (0, 0))).reshape(b, nb, MOBA_BLOCK, H_B, D_B)
    kmean = kp.astype(jnp.float32).mean(2).astype(k.dtype)
    nqb = s // Q_BLOCK
    qb = q.reshape(b * nqb, Q_BLOCK, H_B, D_B)
    b_idx = jnp.repeat(jnp.arange(b), nqb)
    i_idx = jnp.tile(jnp.arange(nqb), b)

    def step(a):
        qq, bi, i = a
        start = i * Q_BLOCK
        own = start // MOBA_BLOCK
        kb_b, vb_b = kp[bi], vp[bi]
        return moba_core(qq, kb_b, vb_b, kmean[bi], kb_b[own], vb_b[own],
                         start + jnp.arange(Q_BLOCK), own * MOBA_BLOCK, own, slopes)

    o = lax.map(step, (qb, b_idx, i_idx))
    return o.reshape(b, s, H_B, D_B)


def moba_sample(q, k_new, v_new, pool_k, pool_v, page_table, layer, slopes):
    t = q.shape[1]
    past = page_table.shape[1] * pool_k.shape[2]
    n_past = past // MOBA_BLOCK
    total = past + t
    nb = -(-total // MOBA_BLOCK)
    pad = nb * MOBA_BLOCK - total
    q_pos = past + jnp.arange(t)

    def one(a):
        qq, kn, vn, pt = a
        kall = jnp.concatenate([pool_k[layer, pt].reshape(past, H_B, D_B), kn], 0)
        vall = jnp.concatenate([pool_v[layer, pt].reshape(past, H_B, D_B), vn], 0)
        kb = jnp.pad(kall, ((0, pad), (0, 0), (0, 0))).reshape(nb, MOBA_BLOCK, H_B, D_B)
        vb = jnp.pad(vall, ((0, pad), (0, 0), (0, 0))).reshape(nb, MOBA_BLOCK, H_B, D_B)
        kmean = kb.astype(jnp.float32).mean(1).astype(kb.dtype)
        return moba_core(qq, kb, vb, kmean, kb[n_past], vb[n_past], q_pos,
                         n_past * MOBA_BLOCK, n_past, slopes)

    return lax.map(one, (q, k_new, v_new, page_table))


def mem_attend(q, mk, mv):
    s = jnp.einsum('bthd,bmhd->bhtm', q, mk).astype(jnp.float32) * (D_M ** -0.5)
    p = jax.nn.softmax(s, axis=-1).astype(q.dtype)
    return jnp.einsum('bhtm,bmhd->bthd', p, mv)


def conv_ffn(x, prev, w_up, conv_w, conv_b, w_down):
    t = x.shape[1]
    u = x @ w_up
    g, val = u[..., :D_FF], u[..., D_FF:]
    gp = jnp.concatenate([prev.astype(g.dtype), g], 1)
    c = conv_b
    for j in range(CONV_W):
        c = c + conv_w[j] * gp[:, j:j + t]
    h = jax.nn.gelu(c) * val
    return h @ w_down, gp[:, t:]


def layer(x, pos, lw, mla_fn, moba_fn, mem_k, mem_v, conv_prev):
    b, t, _ = x.shape
    z = x @ lw['w_in']
    o1 = Q_LORA
    o2 = o1 + KV_LORA
    o3 = o2 + QK_ROPE
    o4 = o3 + H_B * D_B
    o5 = o4 + H_B * D_B
    o6 = o5 + H_B * D_B
    cq, ckv_raw, kpe_raw = z[..., :o1], z[..., o1:o2], z[..., o2:o3]
    qb = z[..., o3:o4].reshape(b, t, H_B, D_B)
    kb = z[..., o4:o5].reshape(b, t, H_B, D_B)
    vb = z[..., o5:o6].reshape(b, t, H_B, D_B)
    qm = z[..., o6:].reshape(b, t, H_M, D_M)
    q = (rms_norm(cq, lw['mla_q_norm']) @ lw['w_uq']).reshape(b, t, H_A, QK_NOPE + QK_ROPE)
    q_nope, q_pe = q[..., :QK_NOPE], rope(q[..., QK_NOPE:], pos)
    q_lat = jnp.einsum('bthn,rhn->bthr', q_nope, lw['w_uk'])
    ckv = rms_norm(ckv_raw, lw['mla_kv_norm'])
    kpe = rope(kpe_raw, pos)
    o_lat = mla_fn(q_lat, q_pe, ckv, kpe)
    o_a = jnp.einsum('bthr,rhv->bthv', o_lat, lw['w_uv']).reshape(b, t, H_A * V_A)
    o_b = moba_fn(qb, kb, vb).reshape(b, t, H_B * D_B)
    o_m = mem_attend(qm, mem_k, mem_v).reshape(b, t, H_M * D_M)
    att = jnp.concatenate([o_a, o_b, o_m], -1) @ lw['w_out']
    x = layer_norm(ALPHA * x + att, lw['ln1_g'], lw['ln1_b'])
    f, conv_new = conv_ffn(x, conv_prev, lw['w_up'], lw['conv_w'], lw['conv_b'], lw['w_down'])
    x = layer_norm(ALPHA * x + f, lw['ln2_g'], lw['ln2_b'])
    return x, ckv, kpe, kb, vb, conv_new


def setup_inputs(seed: int = 0) -> dict:
    key = jax.random.key(seed)
    ks = jax.random.split(key, 32)
    n_pages = PAST_LEN // PAGE_SIZE
    n_phys = (DEC_BATCH * n_pages * 5) // 4

    def nrm(k, shape, scale=1.0):
        return jax.random.normal(k, shape, jnp.float32) * scale

    page_table = jax.random.permutation(ks[10], n_phys)[: DEC_BATCH * n_pages]
    page_table = page_table.reshape(DEC_BATCH, n_pages).astype(jnp.int32)
    return {
        'x_prompt': nrm(ks[0], (BATCH, SEQ, D_MODEL)),
        'x_sample': nrm(ks[1], (DEC_BATCH, DEC_SEQ, D_MODEL)),
        'cache_mla_ckv': nrm(ks[2], (DEPTH, n_phys, PAGE_SIZE, KV_LORA)),
        'cache_mla_kpe': nrm(ks[3], (DEPTH, n_phys, PAGE_SIZE, QK_ROPE)),
        'cache_moba_k': nrm(ks[4], (DEPTH, n_phys, PAGE_SIZE, H_B, D_B)),
        'cache_moba_v': nrm(ks[5], (DEPTH, n_phys, PAGE_SIZE, H_B, D_B)),
        'cache_mem_k': nrm(ks[6], (DEPTH, DEC_BATCH, MEM_LEN, H_M, D_M)),
        'cache_mem_v': nrm(ks[7], (DEPTH, DEC_BATCH, MEM_LEN, H_M, D_M)),
        'state_conv': nrm(ks[8], (DEPTH, DEC_BATCH, CONV_W - 1, D_FF)),
        'page_table': page_table,
        'mem_prompt': nrm(ks[9], (BATCH, MEM_LEN, D_MODEL)),
        'w_in': nrm(ks[11], (DEPTH, D_MODEL, N_IN), D_MODEL ** -0.5),
        'mla_q_norm': 1.0 + nrm(ks[12], (DEPTH, Q_LORA), 0.05),
        'mla_kv_norm': 1.0 + nrm(ks[13], (DEPTH, KV_LORA), 0.05),
        'w_uq': nrm(ks[14], (DEPTH, Q_LORA, H_A * (QK_NOPE + QK_ROPE)), Q_LORA ** -0.5),
        'w_uk': nrm(ks[15], (DEPTH, KV_LORA, H_A, QK_NOPE), KV_LORA ** -0.5),
        'w_uv': nrm(ks[16], (DEPTH, KV_LORA, H_A, V_A), KV_LORA ** -0.5),
        'w_mem_kv': nrm(ks[17], (DEPTH, D_MODEL, 2 * H_M * D_M), D_MODEL ** -0.5),
        'w_out': nrm(ks[18], (DEPTH, MIX, D_MODEL), MIX ** -0.5 * BETA),
        'ln1_g': 1.0 + nrm(ks[19], (DEPTH, D_MODEL), 0.05),
        'ln1_b': nrm(ks[20], (DEPTH, D_MODEL), 0.02),
        'w_up': nrm(ks[21], (DEPTH, D_MODEL, 2 * D_FF), D_MODEL ** -0.5),
        'conv_w': nrm(ks[22], (DEPTH, CONV_W, D_FF), CONV_W ** -0.5),
        'conv_b': nrm(ks[23], (DEPTH, D_FF), 0.02),
        'w_down': nrm(ks[24], (DEPTH, D_FF, D_MODEL), D_FF ** -0.5 * BETA),
        'ln2_g': 1.0 + nrm(ks[25], (DEPTH, D_MODEL), 0.05),
        'ln2_b': nrm(ks[26], (DEPTH, D_MODEL), 0.02),
    }


def reference(x_prompt, x_sample, cache_mla_ckv, cache_mla_kpe, cache_moba_k, cache_moba_v,
              cache_mem_k, cache_mem_v, state_conv, page_table, mem_prompt,
              w_in, mla_q_norm, mla_kv_norm, w_uq, w_uk, w_uv, w_mem_kv, w_out,
              ln1_g, ln1_b, w_up, conv_w, conv_b, w_down, ln2_g, ln2_b):
    slopes = alibi_slopes(H_B)
    b, s = x_prompt.shape[:2]
    t_s = x_sample.shape[1]
    past_len = page_table.shape[1] * cache_mla_ckv.shape[2]
    pos_p = jnp.arange(s)
    pos_s = past_len + jnp.arange(t_s)
    hp, hs = x_prompt, x_sample
    p_ckv, p_kpe, p_k, p_v, p_mk, p_mv, p_cv = [], [], [], [], [], [], []
    s_ckv, s_kpe, s_k, s_v, s_cv = [], [], [], [], []
    moba_p_fn = functools.partial(moba_prompt, slopes=slopes)
    for l in range(DEPTH):
        lw = {'w_in': w_in[l], 'mla_q_norm': mla_q_norm[l], 'mla_kv_norm': mla_kv_norm[l],
              'w_uq': w_uq[l], 'w_uk': w_uk[l], 'w_uv': w_uv[l], 'w_out': w_out[l],
              'ln1_g': ln1_g[l], 'ln1_b': ln1_b[l], 'w_up': w_up[l], 'conv_w': conv_w[l],
              'conv_b': conv_b[l], 'w_down': w_down[l], 'ln2_g': ln2_g[l], 'ln2_b': ln2_b[l]}
        mkv = jnp.einsum('bmd,de->bme', mem_prompt, w_mem_kv[l]).reshape(b, MEM_LEN, 2, H_M, D_M)
        mk, mv = mkv[:, :, 0], mkv[:, :, 1]
        conv0 = jnp.zeros((b, CONV_W - 1, D_FF), hp.dtype)
        hp, ckv, kpe, kb, vb, cv = layer(hp, pos_p, lw, mla_prompt, moba_p_fn, mk, mv, conv0)
        p_ckv.append(ckv); p_kpe.append(kpe); p_k.append(kb); p_v.append(vb)
        p_mk.append(mk); p_mv.append(mv); p_cv.append(cv)
        mla_s_fn = functools.partial(mla_sample, pool_ckv=cache_mla_ckv, pool_kpe=cache_mla_kpe,
                                     page_table=page_table, layer=l)
        moba_s_fn = functools.partial(moba_sample, pool_k=cache_moba_k, pool_v=cache_moba_v,
                                      page_table=page_table, layer=l, slopes=slopes)
        hs, ckv, kpe, kb, vb, cv = layer(hs, pos_s, lw, mla_s_fn, moba_s_fn,
                                         cache_mem_k[l], cache_mem_v[l], state_conv[l])
        s_ckv.append(ckv); s_kpe.append(kpe); s_k.append(kb); s_v.append(vb); s_cv.append(cv)
    return (hp, hs,
            jnp.stack(p_ckv), jnp.stack(p_kpe), jnp.stack(p_k), jnp.stack(p_v),
            jnp.stack(p_mk), jnp.stack(p_mv), jnp.stack(p_cv),
            jnp.stack(s_ckv), jnp.stack(s_kpe), jnp.stack(s_k), jnp.stack(s_v), jnp.stack(s_cv))
```

```python
import functools

import jax
import jax.numpy as jnp
from jax import lax
from jax.experimental import pallas as pl
from jax.experimental.pallas import tpu as pltpu

F32 = jnp.float32
BF16 = jnp.bfloat16

H_A, QK_NOPE, QK_ROPE, V_A = 8, 64, 32, 64
Q_LORA, KV_LORA = 384, 256
H_B, D_B, MOBA_BLOCK, MOBA_TOPK = 4, 64, 256, 3
H_M, D_M = 4, 64
D_FF, CONV_W = 2816, 3
ROPE_THETA = 10000.0
NORM_EPS = 1e-5
RMS_EPS = 1e-6
NEG = -1e30

LANES = 128
SUBLANES = 8
VMEM_LIMIT = 56 * 1024 * 1024

ROPE_HALF = QK_ROPE // 2
AUG0 = 64
ONEHOT0 = 80
MAX_MOBA_BLOCKS = 16

C_CQ, C_CKV, C_QB, C_KB, C_VB, C_QM, C_KPA, C_KPB, C_END = 0, 384, 640, 896, 1152, 1408, 1664, 1792, 1920


def _cparams(n_axes):
    return pltpu.CompilerParams(dimension_semantics=("arbitrary",) * n_axes, vmem_limit_bytes=VMEM_LIMIT)


def _const_spec(shape):
    nd = len(shape)
    return pl.BlockSpec(shape, lambda *_: (0,) * nd, pipeline_mode=pl.Buffered(1))


def _dot(a, b):
    return jnp.dot(a, b, preferred_element_type=F32)


def _dot_nt(a, b):
    return lax.dot_general(a, b, (((1,), (1,)), ((), ())), preferred_element_type=F32)


def _iota(shape, dim):
    return lax.broadcasted_iota(jnp.int32, shape, dim)


def _rms(x, g):
    return x * lax.rsqrt(jnp.mean(x * x, axis=-1, keepdims=True) + RMS_EPS) * g


def _layer_norm(x, g, b):
    mu = jnp.mean(x, axis=-1, keepdims=True)
    xc = x - mu
    var = jnp.mean(xc * xc, axis=-1, keepdims=True)
    return xc * lax.rsqrt(var + NORM_EPS) * g + b


def _gelu_tanh(x):
    return x * (0.5 * (1.0 + jnp.tanh(0.7978845608028654 * (x + 0.044715 * (x * x * x)))))


def _head_to_lanes(seg, h):
    blk = seg[:, (h // 2) * LANES:(h // 2 + 1) * LANES]
    if h % 2:
        blk = pltpu.roll(blk, 64, axis=1)
    return jnp.where(_iota(blk.shape, 1) < 64, blk, 0.0)


def _slope(h):
    return 2.0 ** (-8.0 * (h + 1) / H_B)


def _proj_common(x_ref, win_ref, qn_ref, kvn_ref, wa_ref, wb_ref, c_ref, s_ref, ckv_ref, kpe_ref, kb_ref, vb_ref, qa_ref):
    xb = x_ref[...].astype(BF16)

    def seg(a, b):
        return _dot(xb, win_ref[:, a:b])

    cos_t, sin_t = c_ref[...], s_ref[...]
    scale_a = (QK_NOPE + QK_ROPE) ** -0.5
    cqn = _rms(seg(C_CQ, C_CKV), qn_ref[...]).astype(BF16)
    for h in range(H_A):
        hs = slice(h * LANES, (h + 1) * LANES)
        q_rot = _dot(cqn, wa_ref[:, hs]) * cos_t + _dot(cqn, wb_ref[:, hs]) * sin_t
        qa_ref[:, hs] = (q_rot * scale_a).astype(BF16)
    ckvn = _rms(seg(C_CKV, C_QB), kvn_ref[...])
    ckv_ref[...] = ckvn
    kpe_rot = seg(C_KPA, C_KPB) * cos_t + seg(C_KPB, C_END) * sin_t
    kpe_ref[...] = kpe_rot[:, :QK_ROPE]
    qb, kb, vb, qm = seg(C_QB, C_KB), seg(C_KB, C_VB), seg(C_VB, C_QM), seg(C_QM, C_KPA)
    kb_ref[...] = kb
    vb_ref[...] = vb
    return ckvn.astype(BF16), kpe_rot, qb, kb, vb, qm


def _proj_prompt_kernel(x_ref, win_ref, qn_ref, kvn_ref, wa_ref, wb_ref, wk_ref, wv_ref, c_ref, s_ref,
                        ckv_ref, kpe_ref, kb_ref, vb_ref, qa_ref, ka_ref, va_ref, qbp_ref, kbp_ref, vbb_ref, qmp_ref,
                        *, tm, tiles_per_seq):
    ckvb, kpe_rot, qb, kb, vb, qm = _proj_common(x_ref, win_ref, qn_ref, kvn_ref, wa_ref, wb_ref, c_ref, s_ref,
                                                 ckv_ref, kpe_ref, kb_ref, vb_ref, qa_ref)
    for h in range(H_A):
        hs = slice(h * LANES, (h + 1) * LANES)
        ka_ref[:, hs] = (_dot(ckvb, wk_ref[:, hs]) + kpe_rot).astype(BF16)
    va_ref[...] = _dot(ckvb, wv_ref[...]).astype(BF16)
    vbb_ref[...] = vb.astype(BF16)
    pos = (pl.program_id(0) % tiles_per_seq) * tm + _iota((tm, LANES), 0)
    lane = _iota((tm, LANES), 1)
    blk = (pos >> 8).astype(F32)
    rem = (pos & (MOBA_BLOCK - 1)).astype(F32)
    k_aug = jnp.where(lane < AUG0 + 2, 1.0, jnp.where(lane == AUG0 + 2, blk, rem))
    k_aug = jnp.where((lane >= AUG0) & (lane < AUG0 + 4), k_aug, 0.0)
    k_aug = jnp.where((lane - ONEHOT0) == (pos >> 8), 1.0, k_aug)
    q_aug = jnp.where(lane == AUG0, -float(MOBA_BLOCK) * blk,
                      jnp.where(lane == AUG0 + 1, -rem, jnp.where(lane == AUG0 + 2, float(MOBA_BLOCK), 1.0)))
    q_aug = jnp.where((lane >= AUG0) & (lane < AUG0 + 4), q_aug, 0.0)
    for h in range(H_B):
        hs = slice(h * LANES, (h + 1) * LANES)
        qbp_ref[:, hs] = (_head_to_lanes(qb, h) * (D_B ** -0.5) + _slope(h) * q_aug).astype(BF16)
        kbp_ref[:, hs] = (_head_to_lanes(kb, h) + k_aug).astype(BF16)
        qmp_ref[:, hs] = (_head_to_lanes(qm, h) * (D_M ** -0.5)).astype(BF16)


def _proj_sample_kernel(x_ref, win_ref, qn_ref, kvn_ref, wa_ref, wb_ref, wql_ref, c_ref, s_ref,
                        ckv_ref, kpe_ref, kb_ref, vb_ref, qa_ref, qlat_ref, qb_ref, qm_ref):
    _, _, qb, _, _, qm = _proj_common(x_ref, win_ref, qn_ref, kvn_ref, wa_ref, wb_ref, c_ref, s_ref,
                                      ckv_ref, kpe_ref, kb_ref, vb_ref, qa_ref)
    for h in range(H_A):
        qlat_ref[:, h * KV_LORA:(h + 1) * KV_LORA] = _dot(qa_ref[:, h * LANES:(h + 1) * LANES], wql_ref[h]).astype(BF16)
    qb_ref[...] = qb * (D_B ** -0.5)
    qm_ref[...] = qm * (D_M ** -0.5)


def _proj_specs(tm, n_tab):
    row = lambda w: pl.BlockSpec((tm, w), lambda i: (i, 0))
    tab = pl.BlockSpec((tm, LANES), lambda i: (i % n_tab, 0))
    return row, tab


def _proj_prompt(x2, w, cos_t, sin_t, seq_len):
    n, d = x2.shape
    tm = min(512, seq_len)
    tiles_per_seq = seq_len // tm
    row, tab = _proj_specs(tm, tiles_per_seq)
    out_w = [(KV_LORA, F32), (QK_ROPE, F32), (H_B * D_B, F32), (H_B * D_B, F32), (H_A * LANES, BF16), (H_A * LANES, BF16),
             (H_A * V_A, BF16), (H_B * LANES, BF16), (H_B * LANES, BF16), (H_B * D_B, BF16), (H_M * LANES, BF16)]
    return pl.pallas_call(
        functools.partial(_proj_prompt_kernel, tm=tm, tiles_per_seq=tiles_per_seq),
        grid=(n // tm,),
        in_specs=[row(d), _const_spec(w["win"].shape), _const_spec((1, Q_LORA)), _const_spec((1, KV_LORA)),
                  _const_spec(w["wa"].shape), _const_spec(w["wb"].shape), _const_spec(w["wk"].shape),
                  _const_spec(w["wv"].shape), tab, tab],
        out_specs=[row(wd) for wd, _ in out_w],
        out_shape=[jax.ShapeDtypeStruct((n, wd), dt) for wd, dt in out_w],
        compiler_params=_cparams(1), name="proj_prompt",
    )(x2, w["win"], w["qn"], w["kvn"], w["wa"], w["wb"], w["wk"], w["wv"], cos_t, sin_t)


def _proj_sample(x2, w, cos_t, sin_t):
    n, d = x2.shape
    tm = min(512, n)
    row, tab = _proj_specs(tm, cos_t.shape[0] // tm)
    out_w = [(KV_LORA, F32), (QK_ROPE, F32), (H_B * D_B, F32), (H_B * D_B, F32), (H_A * LANES, BF16),
             (H_A * KV_LORA, BF16), (H_B * D_B, F32), (H_M * D_M, F32)]
    return pl.pallas_call(
        _proj_sample_kernel,
        grid=(n // tm,),
        in_specs=[row(d), _const_spec(w["win"].shape), _const_spec((1, Q_LORA)), _const_spec((1, KV_LORA)),
                  _const_spec(w["wa"].shape), _const_spec(w["wb"].shape), _const_spec(w["wql"].shape), tab, tab],
        out_specs=[row(wd) for wd, _ in out_w],
        out_shape=[jax.ShapeDtypeStruct((n, wd), dt) for wd, dt in out_w],
        compiler_params=_cparams(1), name="proj_sample",
    )(x2, w["win"], w["qn"], w["kvn"], w["wa"], w["wb"], w["wql"], cos_t, sin_t)


def _memkv_kernel(x_ref, w_ref, mk_ref, mv_ref, mkp_ref, mvb_ref):
    kv = _dot(x_ref[...].astype(BF16), w_ref[...])
    mk, mv = kv[:, :H_M * D_M], kv[:, H_M * D_M:]
    mk_ref[...] = mk
    mv_ref[...] = mv
    mvb_ref[...] = mv.astype(BF16)
    for h in range(H_M):
        mkp_ref[:, h * LANES:(h + 1) * LANES] = _head_to_lanes(mk, h).astype(BF16)


def _memkv(mem2, w_bf):
    n, d = mem2.shape
    tm = min(256, n)
    row = lambda w: pl.BlockSpec((tm, w), lambda i: (i, 0))
    out_w = [(H_M * D_M, F32), (H_M * D_M, F32), (H_M * LANES, BF16), (H_M * D_M, BF16)]
    return pl.pallas_call(
        _memkv_kernel, grid=(n // tm,),
        in_specs=[row(d), _const_spec(w_bf.shape)],
        out_specs=[row(wd) for wd, _ in out_w],
        out_shape=[jax.ShapeDtypeStruct((n, wd), dt) for wd, dt in out_w],
        compiler_params=_cparams(1), name="mem_kv",
    )(mem2, w_bf)


def _moba_select(q, g, qi):
    lane = _iota(g.shape, 1)
    c = lane - ONEHOT0
    in_blk = (c >= 0) & (c < MAX_MOBA_BLOCKS)
    valid = in_blk & (c < qi)
    gm = jnp.where(valid, g, -jnp.inf)
    cnt = jnp.zeros(g.shape, F32)
    for cp in range(MAX_MOBA_BLOCKS):
        col = gm[:, ONEHOT0 + cp:ONEHOT0 + cp + 1]
        ahead = jnp.where(col > gm, 1.0, jnp.where(col == gm, jnp.where(cp < c, 1.0, 0.0), 0.0))
        cnt = cnt + ahead
    keep = jnp.where(valid, jnp.where(cnt < MOBA_TOPK, 0.0, NEG), jnp.where(c == qi, 0.0, NEG))
    return jnp.where(in_blk, keep.astype(q.dtype), q)


def _flash_kernel(*refs, tq, tk, mode, n_kblocks):
    if mode == "moba":
        q_ref, k_ref, v_ref, avg_ref, o_ref, km_ref = refs
    else:
        q_ref, k_ref, v_ref, o_ref = refs
    qi = pl.program_id(2)
    heads = []
    for hh in range(2):
        hs = slice(hh * LANES, (hh + 1) * LANES)
        q = q_ref[0, :, hs]
        if mode == "moba":
            @pl.when(qi == 0)
            def _():
                km = _dot(avg_ref[...], k_ref[0, :, hs])
                km_ref[hh] = jnp.where(_iota(km.shape, 1) < D_B, km, 0.0).astype(BF16)

            q = _moba_select(q, _dot_nt(q, km_ref[hh]), qi)

        def block(j, carry, masked):
            m, l, acc = carry
            start = pl.multiple_of(j * tk, tk)
            s = _dot_nt(q, k_ref[0, pl.ds(start, tk), hs])
            if masked:
                s = jnp.where(_iota(s.shape, 0) >= _iota(s.shape, 1), s, NEG)
            m_new = jnp.maximum(m, jnp.max(s, axis=1, keepdims=True))
            a = jnp.exp(m - m_new)
            p = jnp.exp(s - m_new)
            l = a * l + jnp.sum(p, axis=1, keepdims=True)
            acc = a * acc + _dot(p.astype(BF16), v_ref[0, pl.ds(start, tk), :])
            return m_new, l, acc

        carry = (jnp.full((tq, 1), NEG, F32), jnp.zeros((tq, 1), F32), jnp.zeros((tq, LANES), F32))
        if mode == "full":
            carry = lax.fori_loop(0, n_kblocks, lambda j, c: block(j, c, False), carry)
        else:
            carry = block(qi, carry, True)
            carry = lax.fori_loop(0, qi, lambda j, c: block(j, c, False), carry)
        _, l, acc = carry
        heads.append(acc / l)
    o_ref[0] = jnp.where(_iota((tq, LANES), 1) < 64, heads[0], heads[1]).astype(o_ref.dtype)


def _flash(q, k, v, mode, avg=None):
    b, s, w = q.shape
    sk = k.shape[1]
    pairs = w // (2 * LANES)
    tq = tk = min(MOBA_BLOCK, s) if mode != "full" else min(256, s)
    if mode == "full":
        tk = min(256, sk)
    in_specs = [pl.BlockSpec((1, tq, 2 * LANES), lambda bi, p, i: (bi, i, p)),
                pl.BlockSpec((1, sk, 2 * LANES), lambda bi, p, i: (bi, 0, p)),
                pl.BlockSpec((1, sk, LANES), lambda bi, p, i: (bi, 0, p))]
    args = [q, k, v]
    scratch = []
    if mode == "moba":
        in_specs.append(_const_spec(avg.shape))
        args.append(avg)
        scratch = [pltpu.VMEM((2, LANES, LANES), BF16)]
    return pl.pallas_call(
        functools.partial(_flash_kernel, tq=tq, tk=tk, mode=mode, n_kblocks=sk // tk),
        grid=(b, pairs, s // tq),
        in_specs=in_specs,
        out_specs=pl.BlockSpec((1, tq, LANES), lambda bi, p, i: (bi, i, p)),
        out_shape=jax.ShapeDtypeStruct((b, s, pairs * LANES), BF16),
        scratch_shapes=scratch,
        compiler_params=_cparams(3), name="flash_" + mode,
    )(*args)


def _mla_sample_kernel(pt_ref, qlat_ref, qpe_ref, cn_ref, kn_ref, ckv_hbm, kpt_hbm, o_ref,
                       ckv_buf, kpt_buf, sem, *, layer, n_pages, page, tk, n_tok):
    s = pl.program_id(0)
    ns = pl.num_programs(0)
    slot = s % 2
    rows = n_tok * H_A

    def copies(seq, sl, p):
        pg = pt_ref[seq, p]
        off = pl.multiple_of(p * page, page)
        return (pltpu.make_async_copy(ckv_hbm.at[layer, pg], ckv_buf.at[sl, pl.ds(off, page), :], sem.at[sl, 0]),
                pltpu.make_async_copy(kpt_hbm.at[layer, pg], kpt_buf.at[sl, :, pl.ds(off, page)], sem.at[sl, 1]))

    def start_seq(seq, sl):
        def body(p, _):
            for c in copies(seq, sl, p):
                c.start()
            return 0
        lax.fori_loop(0, n_pages, body, 0)

    @pl.when(s == 0)
    def _():
        start_seq(0, 0)

    @pl.when(s + 1 < ns)
    def _():
        start_seq(s + 1, 1 - slot)

    def wait_body(p, _):
        for c in copies(s, slot, p):
            c.wait()
        return 0
    lax.fori_loop(0, n_pages, wait_body, 0)

    ql = qlat_ref[0]
    qp = qpe_ref[0][:, :QK_ROPE]

    def update(carry, sc, vals):
        m, l, acc = carry
        m_new = jnp.maximum(m, jnp.max(sc, axis=1, keepdims=True))
        a = jnp.exp(m - m_new)
        p = jnp.exp(sc - m_new)
        return m_new, a * l + jnp.sum(p, axis=1, keepdims=True), a * acc + _dot(p.astype(BF16), vals)

    pad = jnp.zeros((LANES - n_tok, KV_LORA), F32)
    cn = jnp.concatenate([cn_ref[0], pad], axis=0).astype(BF16)
    kn = jnp.concatenate([kn_ref[0], pad[:, :QK_ROPE]], axis=0).astype(BF16)
    s0 = _dot_nt(ql, cn) + _dot_nt(qp, kn)
    tok = _iota(s0.shape, 0) // H_A
    s0 = jnp.where(_iota(s0.shape, 1) <= tok, s0, NEG)
    carry = (jnp.full((rows, 1), NEG, F32), jnp.zeros((rows, 1), F32), jnp.zeros((rows, KV_LORA), F32))
    carry = update(carry, s0, cn)

    def chunk(c, carry):
        off = pl.multiple_of(c * tk, tk)
        kc = ckv_buf[slot, pl.ds(off, tk), :].astype(BF16)
        kp = kpt_buf[slot, :, pl.ds(off, tk)].astype(BF16)
        return update(carry, _dot_nt(ql, kc) + _dot(qp, kp), kc)

    _, l, acc = lax.fori_loop(0, (n_pages * page) // tk, chunk, carry)
    o_ref[0] = (acc / l).astype(o_ref.dtype)


def _mla_sample(page_table, qlat3, qpe3, ckv_new3, kpe_new3, cache_ckv, cache_kpt, layer):
    ns, n_pages = page_table.shape
    page = cache_ckv.shape[2]
    n_tok = ckv_new3.shape[1]
    rows = n_tok * H_A
    t_past = n_pages * page
    tk = min(512, t_past)
    blk = lambda r, w: pl.BlockSpec((1, r, w), lambda s, pt: (s, 0, 0))
    return pl.pallas_call(
        functools.partial(_mla_sample_kernel, layer=layer, n_pages=n_pages, page=page, tk=tk, n_tok=n_tok),
        grid_spec=pltpu.PrefetchScalarGridSpec(
            num_scalar_prefetch=1, grid=(ns,),
            in_specs=[blk(rows, KV_LORA), blk(rows, LANES), blk(n_tok, KV_LORA), blk(n_tok, QK_ROPE),
                      pl.BlockSpec(memory_space=pl.ANY), pl.BlockSpec(memory_space=pl.ANY)],
            out_specs=blk(rows, KV_LORA),
            scratch_shapes=[pltpu.VMEM((2, t_past, KV_LORA), F32), pltpu.VMEM((2, QK_ROPE, t_past), F32),
                            pltpu.SemaphoreType.DMA((2, 2))]),
        out_shape=jax.ShapeDtypeStruct((ns, rows, KV_LORA), BF16),
        compiler_params=_cparams(1), name="mla_sample",
    )(page_table, qlat3, qpe3, ckv_new3, kpe_new3, cache_ckv, cache_kpt)


def _block_diag_rows(x8):
    head = _iota(x8.shape, 1) // D_B
    return jnp.concatenate([jnp.where(head == h, x8, 0.0) for h in range(H_B)], axis=0)


def _diag_heads(acc, n_tok):
    head = _iota((n_tok, acc.shape[1]), 1) // D_B
    out = jnp.zeros((n_tok, acc.shape[1]), F32)
    for h in range(H_B):
        out = out + jnp.where(head == h, acc[h * n_tok:(h + 1) * n_tok, :], 0.0)
    return out


def _moba_sample_kernel(pt_ref, q_ref, kn_ref, vn_ref, qm_ref, mk_ref, mv_ref, kt_hbm, vt_hbm, ob_ref, om_ref,
                        kt_buf, vt_buf, s_buf, sem, *, layer, n_pages, page, n_tok):
    s = pl.program_id(0)
    ns = pl.num_programs(0)
    slot = s % 2
    rows = n_tok * H_B
    t_past = n_pages * page
    n_blocks = t_past // MOBA_BLOCK

    def copies(seq, sl, p):
        pg = pt_ref[seq, p]
        off = pl.multiple_of(p * page, page)
        return (pltpu.make_async_copy(kt_hbm.at[layer, pg], kt_buf.at[sl, :, :, pl.ds(off, page)], sem.at[sl, 0]),
                pltpu.make_async_copy(vt_hbm.at[layer, pg], vt_buf.at[sl, :, :, pl.ds(off, page)], sem.at[sl, 1]))

    def start_seq(seq, sl):
        def body(p, _):
            for c in copies(seq, sl, p):
                c.start()
            return 0
        lax.fori_loop(0, n_pages, body, 0)

    @pl.when(s == 0)
    def _():
        start_seq(0, 0)

    @pl.when(s + 1 < ns)
    def _():
        start_seq(s + 1, 1 - slot)

    qmb = _block_diag_rows(qm_ref[0]).astype(BF16)
    hd = H_M * D_M
    sm = _dot(qmb, mk_ref[0, 0].reshape(hd, -1).astype(BF16))
    pm = jnp.exp(sm - jnp.max(sm, axis=1, keepdims=True))
    om = _dot_nt(pm.astype(BF16), mv_ref[0, 0].reshape(hd, -1).astype(BF16)) / jnp.sum(pm, axis=1, keepdims=True)
    om_ref[0] = _diag_heads(om, n_tok).astype(om_ref.dtype)

    def wait_body(p, _):
        for c in copies(s, slot, p):
            c.wait()
        return 0
    lax.fori_loop(0, n_pages, wait_body, 0)

    qbd = _block_diag_rows(q_ref[0]).astype(BF16)
    row = _iota((rows, 1), 0)
    r_head, r_tok = row // n_tok, row % n_tok
    slope = jnp.where(r_head == 0, _slope(0), jnp.where(r_head == 1, _slope(1), jnp.where(r_head == 2, _slope(2), _slope(3))))
    lane = _iota((rows, LANES), 1)

    def score_chunk(c, gates):
        off = pl.multiple_of(c * MOBA_BLOCK, MOBA_BLOCK)
        kt = kt_buf[slot, :, :, pl.ds(off, MOBA_BLOCK)].reshape(H_B * D_B, MOBA_BLOCK).astype(BF16)
        sr = _dot(qbd, kt)
        s_buf[:, pl.ds(off, MOBA_BLOCK)] = sr
        return jnp.where(lane == c, jnp.sum(sr, axis=1, keepdims=True), gates)

    gates = lax.fori_loop(0, n_blocks, score_chunk, jnp.zeros((rows, LANES), F32))
    gm = jnp.where(lane < n_blocks, gates, -jnp.inf)
    cnt = jnp.zeros((rows, LANES), F32)
    for cp in range(n_blocks):
        col = gm[:, cp:cp + 1]
        cnt = cnt + jnp.where(col > gm, 1.0, jnp.where(col == gm, jnp.where(cp < lane, 1.0, 0.0), 0.0))
    sel_bias = jnp.where((lane < n_blocks) & (cnt < MOBA_TOPK), 0.0, NEG)

    def update(carry, sc, vt):
        m, l, acc = carry
        m_new = jnp.maximum(m, jnp.max(sc, axis=1, keepdims=True))
        a = jnp.exp(m - m_new)
        p = jnp.exp(sc - m_new)
        return m_new, a * l + jnp.sum(p, axis=1, keepdims=True), a * acc + _dot_nt(p.astype(BF16), vt)

    pad = jnp.zeros((LANES - n_tok, H_B * D_B), F32)
    kn = jnp.concatenate([kn_ref[0], pad], axis=0).astype(BF16)
    vn_t = jnp.concatenate([vn_ref[0], pad], axis=0).T.astype(BF16)
    dist = (r_tok - lane).astype(F32)
    s0 = _dot_nt(qbd, kn) - slope * dist
    s0 = jnp.where(lane <= r_tok, s0, NEG)
    carry = (jnp.full((rows, 1), NEG, F32), jnp.zeros((rows, 1), F32), jnp.zeros((rows, H_B * D_B), F32))
    carry = update(carry, s0, vn_t)

    pos0 = _iota((rows, MOBA_BLOCK), 1)

    def attend_chunk(c, carry):
        off = pl.multiple_of(c * MOBA_BLOCK, MOBA_BLOCK)
        bias_c = jnp.max(jnp.where(lane == c, sel_bias, -jnp.inf), axis=1, keepdims=True)
        dist_c = (t_past + r_tok - (pos0 + c * MOBA_BLOCK)).astype(F32)
        sc = s_buf[:, pl.ds(off, MOBA_BLOCK)] - slope * dist_c + bias_c
        vt = vt_buf[slot, :, :, pl.ds(off, MOBA_BLOCK)].reshape(H_B * D_B, MOBA_BLOCK).astype(BF16)
        return update(carry, sc, vt)

    _, l, acc = lax.fori_loop(0, n_blocks, attend_chunk, carry)
    ob_ref[0] = _diag_heads(acc / l, n_tok).astype(ob_ref.dtype)


def _moba_sample(page_table, q3, kn3, vn3, qm3, mem_kt, mem_vt, cache_kt, cache_vt, layer):
    ns, n_pages = page_table.shape
    page = cache_kt.shape[-1]
    n_tok = q3.shape[1]
    t_past = n_pages * page
    w = H_B * D_B
    mem_len = mem_kt.shape[-1]
    blk = pl.BlockSpec((1, n_tok, w), lambda s, pt: (s, 0, 0))
    mem_blk = pl.BlockSpec((1, 1, H_M, D_M, mem_len), lambda s, pt: (layer, s, 0, 0, 0))
    return pl.pallas_call(
        functools.partial(_moba_sample_kernel, layer=layer, n_pages=n_pages, page=page, n_tok=n_tok),
        grid_spec=pltpu.PrefetchScalarGridSpec(
            num_scalar_prefetch=1, grid=(ns,),
            in_specs=[blk, blk, blk, blk, mem_blk, mem_blk,
                      pl.BlockSpec(memory_space=pl.ANY), pl.BlockSpec(memory_space=pl.ANY)],
            out_specs=[blk, blk],
            scratch_shapes=[pltpu.VMEM((2, H_B, D_B, t_past), F32), pltpu.VMEM((2, H_B, D_B, t_past), F32),
                            pltpu.VMEM((n_tok * H_B, t_past), F32), pltpu.SemaphoreType.DMA((2, 2))]),
        out_shape=[jax.ShapeDtypeStruct((ns, n_tok, w), BF16)] * 2,
        compiler_params=_cparams(1), name="moba_sample",
    )(page_table, q3, kn3, vn3, qm3, mem_kt, mem_vt, cache_kt, cache_vt)


def _outproj_kernel(*refs, alpha, latent):
    if latent:
        x_ref, oa_ref, ob_ref, om_ref, wuv_ref, wo_ref, g_ref, b_ref, y_ref = refs
        oa = _dot(oa_ref[...], wuv_ref[...]).astype(BF16)
    else:
        x_ref, oa_ref, ob_ref, om_ref, wo_ref, g_ref, b_ref, y_ref = refs
        oa = oa_ref[...]
    wa, wb = H_A * V_A, H_A * V_A + H_B * D_B
    att = _dot(oa, wo_ref[:wa, :]) + _dot(ob_ref[...], wo_ref[wa:wb, :]) + _dot(om_ref[...], wo_ref[wb:, :])
    y_ref[...] = _layer_norm(alpha * x_ref[...] + att, g_ref[...], b_ref[...])


def _outproj(x2, oa, ob, om, w, alpha, wuv=None):
    n, d = x2.shape
    tm = min(512, n)
    row = lambda wd: pl.BlockSpec((tm, wd), lambda i: (i, 0))
    in_specs = [row(d), row(oa.shape[1]), row(ob.shape[1]), row(om.shape[1])]
    args = [x2, oa, ob, om]
    if wuv is not None:
        in_specs.append(_const_spec(wuv.shape))
        args.append(wuv)
    in_specs += [_const_spec(w["wo"].shape), _const_spec((1, d)), _const_spec((1, d))]
    args += [w["wo"], w["ln1_g"], w["ln1_b"]]
    return pl.pallas_call(
        functools.partial(_outproj_kernel, alpha=alpha, latent=wuv is not None),
        grid=(n // tm,), in_specs=in_specs, out_specs=row(d),
        out_shape=jax.ShapeDtypeStruct((n, d), F32),
        compiler_params=_cparams(1), name="outproj",
    )(*args)


def _ffn_kernel(*refs, alpha, tm, tf, tiles_per_seq, sample, n_tok):
    if sample:
        x_ref, s1_ref, s2_ref, wg_ref, wv_ref, wd_ref, cw_ref, cb_ref, g_ref, b_ref, y_ref, gate_ref, gbuf = refs
    else:
        x_ref, wg_ref, wv_ref, wd_ref, cw_ref, cb_ref, g_ref, b_ref, y_ref, st_ref, gbuf, carry_ref = refs
        first = (pl.program_id(0) % tiles_per_seq) == 0
    x = x_ref[...]
    xb = x.astype(BF16)
    acc = jnp.zeros(x.shape, F32)
    if sample:
        tpos = _iota((tm, tf), 0) % n_tok
        gbuf[0:SUBLANES, :] = jnp.zeros((SUBLANES, tf), F32)
    for c in range(D_FF // tf):
        cs = slice(c * tf, (c + 1) * tf)
        gate = _dot(xb, wg_ref[:, cs])
        val = _dot(xb, wv_ref[:, cs])
        gbuf[SUBLANES:SUBLANES + tm, :] = gate
        if sample:
            gate_ref[:, cs] = gate
            g1 = jnp.where(tpos >= 1, gbuf[SUBLANES - 1:SUBLANES - 1 + tm, :], s1_ref[:, cs])
            g2 = jnp.where(tpos >= 2, gbuf[SUBLANES - 2:SUBLANES - 2 + tm, :], s2_ref[:, cs])
        else:
            @pl.when(first)
            def _():
                gbuf[0:SUBLANES, :] = jnp.zeros((SUBLANES, tf), F32)

            @pl.when(jnp.logical_not(first))
            def _():
                gbuf[0:SUBLANES, :] = carry_ref[:, cs]

            g1 = gbuf[SUBLANES - 1:SUBLANES - 1 + tm, :]
            g2 = gbuf[SUBLANES - 2:SUBLANES - 2 + tm, :]
            carry_ref[:, cs] = gate[tm - SUBLANES:, :]
            st_ref[0, :, cs] = gate[tm - (CONV_W - 1):, :]
        pre = cb_ref[:, cs] + cw_ref[0:1, cs] * g2 + cw_ref[1:2, cs] * g1 + cw_ref[2:3, cs] * gate
        hid = (_gelu_tanh(pre) * val).astype(BF16)
        acc = acc + _dot(hid, wd_ref[cs, :])
    y_ref[...] = _layer_norm(alpha * x + acc, g_ref[...], b_ref[...])


def _ffn(x2, w, alpha, seq_len, prev=None):
    n, d = x2.shape
    tm = min(256, n) if prev is not None else min(512, seq_len)
    tf = 256
    row = lambda wd: pl.BlockSpec((tm, wd), lambda i: (i, 0))
    wspecs = [_const_spec(w["wg"].shape), _const_spec(w["wv_up"].shape), _const_spec(w["wd"].shape),
              _const_spec((CONV_W, D_FF)), _const_spec((1, D_FF)), _const_spec((1, d)), _const_spec((1, d))]
    wargs = [w["wg"], w["wv_up"], w["wd"], w["cw"], w["cb"], w["ln2_g"], w["ln2_b"]]
    if prev is None:
        tiles_per_seq = seq_len // tm
        n_seq = n // seq_len
        kern = functools.partial(_ffn_kernel, alpha=alpha, tm=tm, tf=tf, tiles_per_seq=tiles_per_seq, sample=False, n_tok=0)
        return pl.pallas_call(
            kern, grid=(n // tm,), in_specs=[row(d)] + wspecs,
            out_specs=[row(d), pl.BlockSpec((1, CONV_W - 1, D_FF), lambda i: (i // tiles_per_seq, 0, 0))],
            out_shape=[jax.ShapeDtypeStruct((n, d), F32), jax.ShapeDtypeStruct((n_seq, CONV_W - 1, D_FF), F32)],
            scratch_shapes=[pltpu.VMEM((tm + SUBLANES, tf), F32), pltpu.VMEM((SUBLANES, D_FF), F32)],
            compiler_params=_cparams(1), name="ffn_prompt",
        )(x2, *wargs)
    s1, s2 = prev
    kern = functools.partial(_ffn_kernel, alpha=alpha, tm=tm, tf=tf, tiles_per_seq=1, sample=True, n_tok=seq_len)
    return pl.pallas_call(
        kern, grid=(n // tm,), in_specs=[row(d), row(D_FF), row(D_FF)] + wspecs,
        out_specs=[row(d), row(D_FF)],
        out_shape=[jax.ShapeDtypeStruct((n, d), F32), jax.ShapeDtypeStruct((n, D_FF), F32)],
        scratch_shapes=[pltpu.VMEM((tm + SUBLANES, tf), F32)],
        compiler_params=_cparams(1), name="ffn_sample",
    )(x2, s1, s2, *wargs)


def _pack_layer(l, w_in, mla_q_norm, mla_kv_norm, w_uq, w_uk, w_uv, w_mem_kv, w_out, ln1_g, ln1_b, w_up, conv_w,
                conv_b, w_down, ln2_g, ln2_b):
    wi = w_in[l]
    d = wi.shape[0]
    o = [0, Q_LORA, Q_LORA + KV_LORA, Q_LORA + KV_LORA + QK_ROPE]
    o += [o[3] + H_B * D_B, o[3] + 2 * H_B * D_B, o[3] + 3 * H_B * D_B]
    cq, ckv, kpe = wi[:, :o[1]], wi[:, o[1]:o[2]], wi[:, o[2]:o[3]]
    qb, kb, vb, qm = wi[:, o[3]:o[4]], wi[:, o[4]:o[5]], wi[:, o[5]:o[6]], wi[:, o[6]:]
    k1, k2 = kpe[:, :ROPE_HALF], kpe[:, ROPE_HALF:]
    zpad = jnp.zeros((d, LANES - QK_ROPE), wi.dtype)
    win = jnp.concatenate([cq, ckv, qb, kb, vb, qm, k1, k2, zpad, k2, k1, zpad], axis=1).astype(BF16)
    uq = w_uq[l].reshape(Q_LORA, H_A, QK_NOPE + QK_ROPE)
    nope, p1, p2 = uq[..., :QK_NOPE], uq[..., QK_NOPE:QK_NOPE + ROPE_HALF], uq[..., QK_NOPE + ROPE_HALF:]
    z32 = jnp.zeros((Q_LORA, H_A, 64 - QK_ROPE), uq.dtype)
    z96 = jnp.zeros((Q_LORA, H_A, LANES - QK_ROPE), uq.dtype)
    wa = jnp.concatenate([p1, p2, z32, nope], axis=2).reshape(Q_LORA, H_A * LANES).astype(BF16)
    wb = jnp.concatenate([p2, p1, z96], axis=2).reshape(Q_LORA, H_A * LANES).astype(BF16)
    uk, uv = w_uk[l], w_uv[l]
    wk = jnp.concatenate([jnp.zeros_like(uk), uk], axis=2).reshape(KV_LORA, H_A * LANES).astype(BF16)
    wv = uv.reshape(KV_LORA, H_A * V_A).astype(BF16)
    uk_t = jnp.transpose(uk, (1, 2, 0))
    wql = jnp.concatenate([jnp.zeros_like(uk_t), uk_t], axis=1).astype(BF16)
    eye = jnp.eye(H_A, dtype=uv.dtype)
    wuv_bd = (eye[:, None, :, None] * jnp.transpose(uv, (1, 0, 2))[:, :, None, :]).reshape(H_A * KV_LORA, H_A * V_A).astype(BF16)
    up = w_up[l]
    return dict(
        win=win, qn=mla_q_norm[l][None], kvn=mla_kv_norm[l][None], wa=wa, wb=wb, wk=wk, wv=wv, wql=wql, wuv_bd=wuv_bd,
        wmem=w_mem_kv[l].astype(BF16), wo=w_out[l].astype(BF16), ln1_g=ln1_g[l][None], ln1_b=ln1_b[l][None],
        wg=up[:, :D_FF].astype(BF16), wv_up=up[:, D_FF:].astype(BF16), wd=w_down[l].astype(BF16),
        cw=conv_w[l], cb=conv_b[l][None], ln2_g=ln2_g[l][None], ln2_b=ln2_b[l][None])


def _rope_tables(pos):
    inv = ROPE_THETA ** (-jnp.arange(0, QK_ROPE, 2, dtype=F32) / QK_ROPE)
    ang = pos.astype(F32)[:, None] * inv
    cos, sin = jnp.cos(ang), jnp.sin(ang)
    n = pos.shape[0]
    ones = jnp.ones((n, LANES - QK_ROPE), F32)
    return (jnp.concatenate([cos, cos, ones], axis=1), jnp.concatenate([-sin, sin, 0.0 * ones], axis=1))


def kernel(x_prompt, x_sample, cache_mla_ckv, cache_mla_kpe, cache_moba_k, cache_moba_v, cache_mem_k, cache_mem_v, state_conv, page_table, mem_prompt, w_in, mla_q_norm, mla_kv_norm, w_uq, w_uk, w_uv, w_mem_kv, w_out, ln1_g, ln1_b, w_up, conv_w, conv_b, w_down, ln2_g, ln2_b):
    b, s, d = x_prompt.shape
    ns, t_s, _ = x_sample.shape
    depth = w_in.shape[0]
    mem_len = mem_prompt.shape[1]
    n_pages, page = page_table.shape[1], cache_mla_ckv.shape[2]
    past = n_pages * page
    assert s % MOBA_BLOCK == 0 and s // MOBA_BLOCK <= MAX_MOBA_BLOCKS and past % MOBA_BLOCK == 0
    assert t_s == SUBLANES and past // MOBA_BLOCK <= LANES
    alpha = (2 * depth) ** 0.25

    cache_kpt = jnp.transpose(cache_mla_kpe, (0, 1, 3, 2))
    cache_kt = jnp.transpose(cache_moba_k, (0, 1, 3, 4, 2))
    cache_vt = jnp.transpose(cache_moba_v, (0, 1, 3, 4, 2))
    mem_kt = jnp.transpose(cache_mem_k, (0, 1, 3, 4, 2))
    mem_vt = jnp.transpose(cache_mem_v, (0, 1, 3, 4, 2))

    cos_p, sin_p = _rope_tables(jnp.arange(s))
    tm_s = min(512, ns * t_s)
    cos_s, sin_s = _rope_tables(jnp.tile(past + jnp.arange(t_s), tm_s // t_s))
    blk_of = jnp.arange(s) // MOBA_BLOCK
    avg = jnp.where((jnp.arange(LANES)[:, None] - ONEHOT0) == blk_of[None, :], 1.0 / MOBA_BLOCK, 0.0).astype(BF16)

    hp = x_prompt.reshape(b * s, d)
    hs = x_sample.reshape(ns * t_s, d)
    mem2 = mem_prompt.reshape(b * mem_len, d)
    outs = {k: [] for k in ("p_ckv", "p_kpe", "p_k", "p_v", "p_mk", "p_mv", "p_cv", "s_ckv", "s_kpe", "s_k", "s_v", "s_cv")}
    for l in range(depth):
        w = _pack_layer(l, w_in, mla_q_norm, mla_kv_norm, w_uq, w_uk, w_uv, w_mem_kv, w_out, ln1_g, ln1_b, w_up,
                        conv_w, conv_b, w_down, ln2_g, ln2_b)
        ckv, kpe, kb, vb, qa, ka, va, qbp, kbp, vbb, qmp = _proj_prompt(hp, w, cos_p, sin_p, s)
        mk, mv, mkp, mvb = _memkv(mem2, w["wmem"])
        r3 = lambda a: a.reshape(b, -1, a.shape[-1])
        o_a = _flash(r3(qa), r3(ka), r3(va), "causal")
        o_b = _flash(r3(qbp), r3(kbp), r3(vbb), "moba", avg)
        o_m = _flash(r3(qmp), r3(mkp), r3(mvb), "full")
        f2 = lambda a: a.reshape(b * s, a.shape[-1])
        hp = _outproj(hp, f2(o_a), f2(o_b), f2(o_m), w, alpha)
        hp, cv = _ffn(hp, w, alpha, s)
        outs["p_ckv"].append(ckv.reshape(b, s, KV_LORA))
        outs["p_kpe"].append(kpe.reshape(b, s, QK_ROPE))
        outs["p_k"].append(kb.reshape(b, s, H_B, D_B))
        outs["p_v"].append(vb.reshape(b, s, H_B, D_B))
        outs["p_mk"].append(mk.reshape(b, mem_len, H_M, D_M))
        outs["p_mv"].append(mv.reshape(b, mem_len, H_M, D_M))
        outs["p_cv"].append(cv)
        ckv, kpe, kb, vb, qa, qlat, qb, qm = _proj_sample(hs, w, cos_s, sin_s)
        o_lat = _mla_sample(page_table, qlat.reshape(ns, t_s * H_A, KV_LORA), qa.reshape(ns, t_s * H_A, LANES),
                            ckv.reshape(ns, t_s, KV_LORA), kpe.reshape(ns, t_s, QK_ROPE), cache_mla_ckv, cache_kpt, l)
        t3 = lambda a: a.reshape(ns, t_s, a.shape[-1])
        o_b, o_m = _moba_sample(page_table, t3(qb), t3(kb), t3(vb), t3(qm), mem_kt, mem_vt, cache_kt, cache_vt, l)
        hs = _outproj(hs, o_lat.reshape(ns * t_s, H_A * KV_LORA), o_b.reshape(ns * t_s, -1), o_m.reshape(ns * t_s, -1),
                      w, alpha, wuv=w["wuv_bd"])
        prev = state_conv[l]
        zrow = jnp.zeros((ns, t_s - 1, D_FF), F32)
        s1 = jnp.concatenate([prev[:, 1:2], zrow], axis=1).reshape(ns * t_s, D_FF)
        s2 = jnp.concatenate([prev, zrow[:, 1:]], axis=1).reshape(ns * t_s, D_FF)
        hs, gate = _ffn(hs, w, alpha, t_s, prev=(s1, s2))
        outs["s_ckv"].append(ckv.reshape(ns, t_s, KV_LORA))
        outs["s_kpe"].append(kpe.reshape(ns, t_s, QK_ROPE))
        outs["s_k"].append(kb.reshape(ns, t_s, H_B, D_B))
        outs["s_v"].append(vb.reshape(ns, t_s, H_B, D_B))
        outs["s_cv"].append(gate.reshape(ns, t_s, D_FF)[:, t_s - (CONV_W - 1):])
    st = lambda k: jnp.stack(outs[k])
    return (hp.reshape(b, s, d), hs.reshape(ns, t_s, d),
            st("p_ckv"), st("p_kpe"), st("p_k"), st("p_v"), st("p_mk"), st("p_mv"), st("p_cv"),
            st("s_ckv"), st("s_kpe"), st("s_k"), st("s_v"), st("s_cv"))
```

```python
import functools

import jax
import jax.numpy as jnp
from jax import lax
from jax.experimental import pallas as pl
from jax.experimental.pallas import tpu as pltpu

F32 = jnp.float32
BF16 = jnp.bfloat16

H_A, QK_NOPE, QK_ROPE, V_A = 8, 64, 32, 64
Q_LORA, KV_LORA = 384, 256
H_B, D_B, MOBA_BLOCK, MOBA_TOPK = 4, 64, 256, 3
H_M, D_M = 4, 64
D_FF, CONV_W = 2816, 3
ROPE_THETA = 10000.0
NORM_EPS = 1e-5
RMS_EPS = 1e-6
NEG = -1e30

LANES = 128
SUBLANES = 8
VMEM_LIMIT = 56 * 1024 * 1024

ROPE_HALF = QK_ROPE // 2
AUG0 = 64
ONEHOT0 = 80
MAX_MOBA_BLOCKS = 16

C_CQ, C_CKV, C_QB, C_KB, C_VB, C_QM, C_KPA, C_KPB, C_END = 0, 384, 640, 896, 1152, 1408, 1664, 1792, 1920


def _cparams(n_axes):
    return pltpu.CompilerParams(dimension_semantics=("arbitrary",) * n_axes, vmem_limit_bytes=VMEM_LIMIT)


def _const_spec(shape):
    nd = len(shape)
    return pl.BlockSpec(shape, lambda *_: (0,) * nd, pipeline_mode=pl.Buffered(1))


def _dot(a, b):
    return jnp.dot(a, b, preferred_element_type=F32)


def _dot_nt(a, b):
    return lax.dot_general(a, b, (((1,), (1,)), ((), ())), preferred_element_type=F32)


def _iota(shape, dim):
    return lax.broadcasted_iota(jnp.int32, shape, dim)


def _rms(x, g):
    return x * lax.rsqrt(jnp.mean(x * x, axis=-1, keepdims=True) + RMS_EPS) * g


def _layer_norm(x, g, b):
    mu = jnp.mean(x, axis=-1, keepdims=True)
    xc = x - mu
    var = jnp.mean(xc * xc, axis=-1, keepdims=True)
    return xc * lax.rsqrt(var + NORM_EPS) * g + b


def _gelu_tanh(x):
    return x * (0.5 * (1.0 + jnp.tanh(0.7978845608028654 * (x + 0.044715 * (x * x * x)))))


def _head_to_lanes(seg, h):
    blk = seg[:, (h // 2) * LANES:(h // 2 + 1) * LANES]
    if h % 2:
        blk = pltpu.roll(blk, 64, axis=1)
    return jnp.where(_iota(blk.shape, 1) < 64, blk, 0.0)


def _slope(h):
    return 2.0 ** (-8.0 * (h + 1) / H_B)


def _proj_common(x_ref, win_ref, qn_ref, kvn_ref, wa_ref, wb_ref, c_ref, s_ref, ckv_ref, kpe_ref, kb_ref, vb_ref, qa_ref):
    xb = x_ref[...].astype(BF16)

    def seg(a, b):
        return _dot(xb, win_ref[:, a:b])

    cos_t, sin_t = c_ref[...], s_ref[...]
    scale_a = (QK_NOPE + QK_ROPE) ** -0.5
    cqn = _rms(seg(C_CQ, C_CKV), qn_ref[...]).astype(BF16)
    for h in range(H_A):
        hs = slice(h * LANES, (h + 1) * LANES)
        q_rot = _dot(cqn, wa_ref[:, hs]) * cos_t + _dot(cqn, wb_ref[:, hs]) * sin_t
        qa_ref[:, hs] = (q_rot * scale_a).astype(BF16)
    ckvn = _rms(seg(C_CKV, C_QB), kvn_ref[...])
    ckv_ref[...] = ckvn
    kpe_rot = seg(C_KPA, C_KPB) * cos_t + seg(C_KPB, C_END) * sin_t
    kpe_ref[...] = kpe_rot[:, :QK_ROPE]
    qb, kb, vb, qm = seg(C_QB, C_KB), seg(C_KB, C_VB), seg(C_VB, C_QM), seg(C_QM, C_KPA)
    kb_ref[...] = kb
    vb_ref[...] = vb
    return ckvn.astype(BF16), kpe_rot, qb, kb, vb, qm


def _proj_prompt_kernel(x_ref, win_ref, qn_ref, kvn_ref, wa_ref, wb_ref, wk_ref, wv_ref, c_ref, s_ref,
                        ckv_ref, kpe_ref, kb_ref, vb_ref, qa_ref, ka_ref, va_ref, qbp_ref, kbp_ref, vbb_ref, qmp_ref,
                        *, tm, tiles_per_seq):
    ckvb, kpe_rot, qb, kb, vb, qm = _proj_common(x_ref, win_ref, qn_ref, kvn_ref, wa_ref, wb_ref, c_ref, s_ref,
                                                 ckv_ref, kpe_ref, kb_ref, vb_ref, qa_ref)
    for h in range(H_A):
        hs = slice(h * LANES, (h + 1) * LANES)
        ka_ref[:, hs] = (_dot(ckvb, wk_ref[:, hs]) + kpe_rot).astype(BF16)
    va_ref[...] = _dot(ckvb, wv_ref[...]).astype(BF16)
    vbb_ref[...] = vb.astype(BF16)
    pos = (pl.program_id(0) % tiles_per_seq) * tm + _iota((tm, LANES), 0)
    lane = _iota((tm, LANES), 1)
    blk = (pos >> 8).astype(F32)
    rem = (pos & (MOBA_BLOCK - 1)).astype(F32)
    k_aug = jnp.where(lane < AUG0 + 2, 1.0, jnp.where(lane == AUG0 + 2, blk, rem))
    k_aug = jnp.where((lane >= AUG0) & (lane < AUG0 + 4), k_aug, 0.0)
    k_aug = jnp.where((lane - ONEHOT0) == (pos >> 8), 1.0, k_aug)
    q_aug = jnp.where(lane == AUG0, -float(MOBA_BLOCK) * blk,
                      jnp.where(lane == AUG0 + 1, -rem, jnp.where(lane == AUG0 + 2, float(MOBA_BLOCK), 1.0)))
    q_aug = jnp.where((lane >= AUG0) & (lane < AUG0 + 4), q_aug, 0.0)
    for h in range(H_B):
        hs = slice(h * LANES, (h + 1) * LANES)
        qbp_ref[:, hs] = (_head_to_lanes(qb, h) * (D_B ** -0.5) + _slope(h) * q_aug).astype(BF16)
        kbp_ref[:, hs] = (_head_to_lanes(kb, h) + k_aug).astype(BF16)
        qmp_ref[:, hs] = (_head_to_lanes(qm, h) * (D_M ** -0.5)).astype(BF16)


def _proj_sample_kernel(x_ref, win_ref, qn_ref, kvn_ref, wa_ref, wb_ref, wql_ref, c_ref, s_ref,
                        ckv_ref, kpe_ref, kb_ref, vb_ref, qa_ref, qlat_ref, qb_ref, qm_ref):
    _, _, qb, _, _, qm = _proj_common(x_ref, win_ref, qn_ref, kvn_ref, wa_ref, wb_ref, c_ref, s_ref,
                                      ckv_ref, kpe_ref, kb_ref, vb_ref, qa_ref)
    for h in range(H_A):
        qlat_ref[:, h * KV_LORA:(h + 1) * KV_LORA] = _dot(qa_ref[:, h * LANES:(h + 1) * LANES], wql_ref[h]).astype(BF16)
    qb_ref[...] = qb * (D_B ** -0.5)
    qm_ref[...] = qm * (D_M ** -0.5)


def _proj_specs(tm, n_tab):
    row = lambda w: pl.BlockSpec((tm, w), lambda i: (i, 0))
    tab = pl.BlockSpec((tm, LANES), lambda i: (i % n_tab, 0))
    return row, tab


def _proj_prompt(x2, w, cos_t, sin_t, seq_len):
    n, d = x2.shape
    tm = min(512, seq_len)
    tiles_per_seq = seq_len // tm
    row, tab = _proj_specs(tm, tiles_per_seq)
    out_w = [(KV_LORA, F32), (QK_ROPE, F32), (H_B * D_B, F32), (H_B * D_B, F32), (H_A * LANES, BF16), (H_A * LANES, BF16),
             (H_A * V_A, BF16), (H_B * LANES, BF16), (H_B * LANES, BF16), (H_B * D_B, BF16), (H_M * LANES, BF16)]
    return pl.pallas_call(
        functools.partial(_proj_prompt_kernel, tm=tm, tiles_per_seq=tiles_per_seq),
        grid=(n // tm,),
        in_specs=[row(d), _const_spec(w["win"].shape), _const_spec((1, Q_LORA)), _const_spec((1, KV_LORA)),
                  _const_spec(w["wa"].shape), _const_spec(w["wb"].shape), _const_spec(w["wk"].shape),
                  _const_spec(w["wv"].shape), tab, tab],
        out_specs=[row(wd) for wd, _ in out_w],
        out_shape=[jax.ShapeDtypeStruct((n, wd), dt) for wd, dt in out_w],
        compiler_params=_cparams(1), name="proj_prompt",
    )(x2, w["win"], w["qn"], w["kvn"], w["wa"], w["wb"], w["wk"], w["wv"], cos_t, sin_t)


def _proj_sample(x2, w, cos_t, sin_t):
    n, d = x2.shape
    tm = min(512, n)
    row, tab = _proj_specs(tm, cos_t.shape[0] // tm)
    out_w = [(KV_LORA, F32), (QK_ROPE, F32), (H_B * D_B, F32), (H_B * D_B, F32), (H_A * LANES, BF16),
             (H_A * KV_LORA, BF16), (H_B * D_B, F32), (H_M * D_M, F32)]
    return pl.pallas_call(
        _proj_sample_kernel,
        grid=(n // tm,),
        in_specs=[row(d), _const_spec(w["win"].shape), _const_spec((1, Q_LORA)), _const_spec((1, KV_LORA)),
                  _const_spec(w["wa"].shape), _const_spec(w["wb"].shape), _const_spec(w["wql"].shape), tab, tab],
        out_specs=[row(wd) for wd, _ in out_w],
        out_shape=[jax.ShapeDtypeStruct((n, wd), dt) for wd, dt in out_w],
        compiler_params=_cparams(1), name="proj_sample",
    )(x2, w["win"], w["qn"], w["kvn"], w["wa"], w["wb"], w["wql"], cos_t, sin_t)


def _memkv_kernel(x_ref, w_ref, mk_ref, mv_ref, mkp_ref, mvb_ref):
    kv = _dot(x_ref[...].astype(BF16), w_ref[...])
    mk, mv = kv[:, :H_M * D_M], kv[:, H_M * D_M:]
    mk_ref[...] = mk
    mv_ref[...] = mv
    mvb_ref[...] = mv.astype(BF16)
    for h in range(H_M):
        mkp_ref[:, h * LANES:(h + 1) * LANES] = _head_to_lanes(mk, h).astype(BF16)


def _memkv(mem2, w_bf):
    n, d = mem2.shape
    tm = min(256, n)
    row = lambda w: pl.BlockSpec((tm, w), lambda i: (i, 0))
    out_w = [(H_M * D_M, F32), (H_M * D_M, F32), (H_M * LANES, BF16), (H_M * D_M, BF16)]
    return pl.pallas_call(
        _memkv_kernel, grid=(n // tm,),
        in_specs=[row(d), _const_spec(w_bf.shape)],
        out_specs=[row(wd) for wd, _ in out_w],
        out_shape=[jax.ShapeDtypeStruct((n, wd), dt) for wd, dt in out_w],
        compiler_params=_cparams(1), name="mem_kv",
    )(mem2, w_bf)


def _moba_bias(q, km, qi):
    tq = q.shape[0]
    gate_t = _dot_nt(km, q)[ONEHOT0:ONEHOT0 + MAX_MOBA_BLOCKS, :]
    c = _iota(gate_t.shape, 0)
    valid = c < qi
    gm = jnp.where(valid, gate_t, -jnp.inf)
    ahead = jnp.zeros(gate_t.shape, F32)
    for cp in range(MAX_MOBA_BLOCKS):
        r = gm[cp:cp + 1, :]
        ahead = ahead + jnp.where(r > gm, 1.0, jnp.where(r == gm, jnp.where(cp < c, 1.0, 0.0), 0.0))
    keep = jnp.where(valid, jnp.where(ahead < MOBA_TOPK, 0.0, NEG), jnp.where(c == qi, 0.0, NEG))
    keep_t = jnp.concatenate([jnp.zeros((ONEHOT0, tq), F32), keep,
                              jnp.zeros((LANES - ONEHOT0 - MAX_MOBA_BLOCKS, tq), F32)], axis=0)
    bias = keep_t.T
    lane = _iota(bias.shape, 1)
    return jnp.where((lane >= ONEHOT0) & (lane < ONEHOT0 + MAX_MOBA_BLOCKS), bias.astype(q.dtype), q)


def _flash_kernel(*refs, tq, tk, mode, n_kblocks):
    if mode == "moba":
        q_ref, k_ref, v_ref, avg_ref, o_ref, s_ref, mx_ref, ls_ref, acc_ref, km_ref = refs
    else:
        q_ref, k_ref, v_ref, o_ref, s_ref, mx_ref, ls_ref, acc_ref = refs
    qi = pl.program_id(2)
    pieces = tk // LANES
    hsl = [slice(hh * LANES, (hh + 1) * LANES) for hh in range(2)]
    qs = []
    for hh in range(2):
        q = q_ref[0, :, hsl[hh]]
        if mode == "moba":
            @pl.when(qi == 0)
            def _():
                km = _dot(avg_ref[...], k_ref[0, :, hsl[hh]])
                km_ref[hh] = jnp.where(_iota(km.shape, 1) < D_B, km, 0.0).astype(BF16)

            q = _moba_bias(q, km_ref[hh], qi)
        qs.append(q)

    def fold(x, op):
        out = x[:, :LANES]
        for i in range(1, pieces):
            out = op(out, x[:, i * LANES:(i + 1) * LANES])
        return out

    def score(slot, j, first):
        src = pl.multiple_of(j * tk, tk)
        dst = pl.multiple_of(slot * tk, tk)
        for hh in range(2):
            s = _dot_nt(qs[hh], k_ref[0, pl.ds(src, tk), hsl[hh]])
            if first and mode != "full":
                s = jnp.where(qi * tq + _iota(s.shape, 0) >= src + _iota(s.shape, 1), s, NEG)
            s_ref[hh, :, pl.ds(dst, tk)] = s
            f = fold(s, jnp.maximum)
            mx_ref[hh] = f if first else jnp.maximum(mx_ref[hh], f)

    def attend(slot, j, first):
        src = pl.multiple_of(j * tk, tk)
        dst = pl.multiple_of(slot * tk, tk)
        for hh in range(2):
            mb = mx_ref[hh]
            ps = [jnp.exp(s_ref[hh, :, pl.ds(dst + i * LANES, LANES)] - mb) for i in range(pieces)]
            psum = ps[0]
            for x in ps[1:]:
                psum = psum + x
            p = jnp.concatenate([x.astype(BF16) for x in ps], axis=1) if pieces > 1 else ps[0].astype(BF16)
            pv = _dot(p, v_ref[0, pl.ds(src, tk), :])
            ls_ref[hh] = psum if first else ls_ref[hh] + psum
            acc_ref[hh] = pv if first else acc_ref[hh] + pv

    def sweep(fn):
        if mode == "full":
            fn(0, 0, True)
            lax.fori_loop(1, n_kblocks, lambda t, c: (fn(t, t, False), c)[1], 0)
        else:
            last = (qi * tq) // tk
            fn(0, last, True)
            lax.fori_loop(1, last + 1, lambda t, c: (fn(t, t - 1, False), c)[1], 0)

    sweep(score)
    for hh in range(2):
        mx_ref[hh] = jnp.broadcast_to(jnp.max(mx_ref[hh], axis=1, keepdims=True), (tq, LANES))
    sweep(attend)
    outs = [acc_ref[hh] / jnp.sum(ls_ref[hh], axis=1, keepdims=True) for hh in range(2)]
    o_ref[0] = jnp.where(_iota((tq, LANES), 1) < 64, outs[0], outs[1]).astype(o_ref.dtype)


def _flash(q, k, v, mode, avg=None):
    b, s, w = q.shape
    sk = k.shape[1]
    pairs = w // (2 * LANES)
    tq = min(MOBA_BLOCK, s)
    tk = min(MOBA_BLOCK, sk) if mode == "full" else min(2 * MOBA_BLOCK, sk)
    assert sk % tk == 0
    in_specs = [pl.BlockSpec((1, tq, 2 * LANES), lambda bi, p, i: (bi, i, p)),
                pl.BlockSpec((1, sk, 2 * LANES), lambda bi, p, i: (bi, 0, p)),
                pl.BlockSpec((1, sk, LANES), lambda bi, p, i: (bi, 0, p))]
    args = [q, k, v]
    scratch = [pltpu.VMEM((2, tq, sk), F32)] + [pltpu.VMEM((2, tq, LANES), F32)] * 3
    if mode == "moba":
        in_specs.append(_const_spec(avg.shape))
        args.append(avg)
        scratch.append(pltpu.VMEM((2, LANES, LANES), BF16))
    return pl.pallas_call(
        functools.partial(_flash_kernel, tq=tq, tk=tk, mode=mode, n_kblocks=sk // tk),
        grid=(b, pairs, s // tq),
        in_specs=in_specs,
        out_specs=pl.BlockSpec((1, tq, LANES), lambda bi, p, i: (bi, i, p)),
        out_shape=jax.ShapeDtypeStruct((b, s, pairs * LANES), BF16),
        scratch_shapes=scratch,
        compiler_params=_cparams(3), name="flash_" + mode,
    )(*args)


def _fold_lanes(x, op):
    out = x[:, :LANES]
    for i in range(1, x.shape[1] // LANES):
        out = op(out, x[:, i * LANES:(i + 1) * LANES])
    return out


def _mla_sample_kernel(pt_ref, qlat_ref, qpe_ref, cn_ref, kn_ref, ckv_hbm, kpt_hbm, o_ref,
                       ckv_buf, kpt_buf, kc_ref, s_ref, sem, *, layer, n_pages, page, tk, n_tok):
    s = pl.program_id(0)
    ns = pl.num_programs(0)
    slot = s % 2
    rows = n_tok * H_A
    n_chunks = (n_pages * page) // tk
    pages_per_chunk = tk // page

    def copies(pg, sl, p):
        off = pl.multiple_of(p * page, page)
        return (pltpu.make_async_copy(ckv_hbm.at[layer, pg], ckv_buf.at[sl, pl.ds(off, page), :], sem.at[sl, 0]),
                pltpu.make_async_copy(kpt_hbm.at[layer, pg], kpt_buf.at[sl, :, pl.ds(off, page)], sem.at[sl, 1]))

    def start_pages(seq, sl, p0, n):
        for i in range(n):
            for c in copies(pt_ref[seq, p0 + i], sl, p0 + i):
                c.start()

    @pl.when(s == 0)
    def _():
        lax.fori_loop(0, n_chunks, lambda c, x: (start_pages(0, 0, c * pages_per_chunk, pages_per_chunk), x)[1], 0)

    for p in range(n_pages):
        for c in copies(0, slot, p):
            c.wait()

    ql = qlat_ref[0]
    qp = qpe_ref[0][:, :QK_ROPE]

    def score_chunk(c, mx):
        @pl.when(s + 1 < ns)
        def _():
            start_pages(s + 1, 1 - slot, c * pages_per_chunk, pages_per_chunk)

        off = pl.multiple_of(c * tk, tk)
        kc = ckv_buf[slot, pl.ds(off, tk), :].astype(BF16)
        kc_ref[pl.ds(off, tk), :] = kc
        sc = _dot_nt(ql, kc) + _dot(qp, kpt_buf[slot, :, pl.ds(off, tk)].astype(BF16))
        s_ref[:, pl.ds(off, tk)] = sc
        return jnp.maximum(mx, _fold_lanes(sc, jnp.maximum))

    pad = jnp.zeros((LANES - n_tok, KV_LORA), F32)
    cn = jnp.concatenate([cn_ref[0], pad], axis=0).astype(BF16)
    kn = jnp.concatenate([kn_ref[0], pad[:, :QK_ROPE]], axis=0).astype(BF16)
    s0 = _dot_nt(ql, cn) + _dot_nt(qp, kn)
    s0 = jnp.where(_iota(s0.shape, 1) <= _iota(s0.shape, 0) // H_A, s0, NEG)
    mx = lax.fori_loop(0, n_chunks, score_chunk, s0, unroll=min(4, n_chunks))
    mb = jnp.broadcast_to(jnp.max(mx, axis=1, keepdims=True), (rows, LANES))

    def attend_chunk(c, carry):
        ls, acc = carry
        off = pl.multiple_of(c * tk, tk)
        ps = [jnp.exp(s_ref[:, pl.ds(off + i * LANES, LANES)] - mb) for i in range(tk // LANES)]
        for x in ps:
            ls = ls + x
        p = jnp.concatenate([x.astype(BF16) for x in ps], axis=1)
        return ls, acc + _dot(p, kc_ref[pl.ds(off, tk), :])

    p0 = jnp.exp(s0 - mb)
    ls, acc = lax.fori_loop(0, n_chunks, attend_chunk, (p0, _dot(p0.astype(BF16), cn)), unroll=min(4, n_chunks))
    o_ref[0] = (acc / jnp.sum(ls, axis=1, keepdims=True)).astype(o_ref.dtype)


def _mla_sample(page_table, qlat3, qpe3, ckv_new3, kpe_new3, cache_ckv, cache_kpt, layer):
    ns, n_pages = page_table.shape
    page = cache_ckv.shape[2]
    n_tok = ckv_new3.shape[1]
    rows = n_tok * H_A
    t_past = n_pages * page
    tk = min(512, t_past)
    blk = lambda r, w: pl.BlockSpec((1, r, w), lambda s, pt: (s, 0, 0))
    return pl.pallas_call(
        functools.partial(_mla_sample_kernel, layer=layer, n_pages=n_pages, page=page, tk=tk, n_tok=n_tok),
        grid_spec=pltpu.PrefetchScalarGridSpec(
            num_scalar_prefetch=1, grid=(ns,),
            in_specs=[blk(rows, KV_LORA), blk(rows, LANES), blk(n_tok, KV_LORA), blk(n_tok, QK_ROPE),
                      pl.BlockSpec(memory_space=pl.ANY), pl.BlockSpec(memory_space=pl.ANY)],
            out_specs=blk(rows, KV_LORA),
            scratch_shapes=[pltpu.VMEM((2, t_past, KV_LORA), F32), pltpu.VMEM((2, QK_ROPE, t_past), F32),
                            pltpu.VMEM((t_past, KV_LORA), BF16), pltpu.VMEM((rows, t_past), F32),
                            pltpu.SemaphoreType.DMA((2, 2))]),
        out_shape=jax.ShapeDtypeStruct((ns, rows, KV_LORA), BF16),
        compiler_params=_cparams(1), name="mla_sample",
    )(page_table, qlat3, qpe3, ckv_new3, kpe_new3, cache_ckv, cache_kpt)


def _block_diag_rows(x8):
    head = _iota(x8.shape, 1) // D_B
    return jnp.concatenate([jnp.where(head == h, x8, 0.0) for h in range(H_B)], axis=0)


def _diag_heads(acc, n_tok):
    head = _iota((n_tok, acc.shape[1]), 1) // D_B
    out = jnp.zeros((n_tok, acc.shape[1]), F32)
    for h in range(H_B):
        out = out + jnp.where(head == h, acc[h * n_tok:(h + 1) * n_tok, :], 0.0)
    return out


def _moba_sample_kernel(pt_ref, q_ref, kn_ref, vn_ref, qm_ref, mk_ref, mv_ref, kt_hbm, vt_hbm, ob_ref, om_ref,
                        kt_buf, vt_buf, s_buf, sem, *, layer, n_pages, page, n_tok):
    s = pl.program_id(0)
    ns = pl.num_programs(0)
    slot = s % 2
    rows = n_tok * H_B
    t_past = n_pages * page
    n_blocks = t_past // MOBA_BLOCK

    pages_per_block = MOBA_BLOCK // page

    def copies(pg, sl, p):
        off = pl.multiple_of(p * page, page)
        return (pltpu.make_async_copy(kt_hbm.at[layer, pg], kt_buf.at[sl, :, :, pl.ds(off, page)], sem.at[sl, 0]),
                pltpu.make_async_copy(vt_hbm.at[layer, pg], vt_buf.at[sl, :, :, pl.ds(off, page)], sem.at[sl, 1]))

    def start_pages(seq, sl, p0, n):
        for i in range(n):
            for c in copies(pt_ref[seq, p0 + i], sl, p0 + i):
                c.start()

    @pl.when(s == 0)
    def _():
        lax.fori_loop(0, n_blocks, lambda c, x: (start_pages(0, 0, c * pages_per_block, pages_per_block), x)[1], 0)

    qmb = _block_diag_rows(qm_ref[0]).astype(BF16)
    hd = H_M * D_M
    sm = _dot(qmb, mk_ref[0, 0].reshape(hd, -1).astype(BF16))
    pm = jnp.exp(sm - jnp.max(sm, axis=1, keepdims=True))
    om = _dot_nt(pm.astype(BF16), mv_ref[0, 0].reshape(hd, -1).astype(BF16)) / jnp.sum(pm, axis=1, keepdims=True)
    om_ref[0] = _diag_heads(om, n_tok).astype(om_ref.dtype)

    for p in range(n_pages):
        for c in copies(0, slot, p):
            c.wait()

    qbd = _block_diag_rows(q_ref[0]).astype(BF16)
    row = _iota((rows, 1), 0)
    r_head, r_tok = row // n_tok, row % n_tok
    slope = jnp.where(r_head == 0, _slope(0), jnp.where(r_head == 1, _slope(1), jnp.where(r_head == 2, _slope(2), _slope(3))))
    lane = _iota((rows, LANES), 1)

    def score_chunk(c, gates):
        @pl.when(s + 1 < ns)
        def _():
            start_pages(s + 1, 1 - slot, c * pages_per_block, pages_per_block)

        off = pl.multiple_of(c * MOBA_BLOCK, MOBA_BLOCK)
        kt = kt_buf[slot, :, :, pl.ds(off, MOBA_BLOCK)].reshape(H_B * D_B, MOBA_BLOCK).astype(BF16)
        sr = _dot(qbd, kt)
        s_buf[:, pl.ds(off, MOBA_BLOCK)] = sr
        return jnp.where(lane == c, jnp.sum(_fold_lanes(sr, jnp.add), axis=1, keepdims=True), gates)

    gates = lax.fori_loop(0, n_blocks, score_chunk, jnp.zeros((rows, LANES), F32), unroll=min(8, n_blocks))
    gm = jnp.where(lane < n_blocks, gates, -jnp.inf)
    ahead = jnp.zeros((rows, LANES), F32)
    for cp in range(n_blocks):
        col = gm[:, cp:cp + 1]
        ahead = ahead + jnp.where(col > gm, 1.0, jnp.where(col == gm, jnp.where(cp < lane, 1.0, 0.0), 0.0))
    sel_bias = jnp.where((lane < n_blocks) & (ahead < MOBA_TOPK), 0.0, NEG)

    pad = jnp.zeros((LANES - n_tok, H_B * D_B), F32)
    kn = jnp.concatenate([kn_ref[0], pad], axis=0).astype(BF16)
    vn_t = jnp.concatenate([vn_ref[0], pad], axis=0).T.astype(BF16)
    s0 = _dot_nt(qbd, kn) - slope * (r_tok - lane).astype(F32)
    s0 = jnp.where(lane <= r_tok, s0, NEG)

    alibi0 = slope * (_iota((rows, MOBA_BLOCK), 1) - r_tok - t_past).astype(F32)
    mx = s0
    for c in range(n_blocks):
        cs = slice(c * MOBA_BLOCK, (c + 1) * MOBA_BLOCK)
        sb = s_buf[:, cs] + alibi0 + (sel_bias[:, c:c + 1] + slope * float(c * MOBA_BLOCK))
        s_buf[:, cs] = sb
        mx = jnp.maximum(mx, _fold_lanes(sb, jnp.maximum))
    mb = jnp.broadcast_to(jnp.max(mx, axis=1, keepdims=True), (rows, LANES))

    def attend_chunk(c, carry):
        ls, acc = carry
        off = pl.multiple_of(c * MOBA_BLOCK, MOBA_BLOCK)
        ps = [jnp.exp(s_buf[:, pl.ds(off + i * LANES, LANES)] - mb) for i in range(MOBA_BLOCK // LANES)]
        for x in ps:
            ls = ls + x
        p = jnp.concatenate([x.astype(BF16) for x in ps], axis=1)
        vt = vt_buf[slot, :, :, pl.ds(off, MOBA_BLOCK)].reshape(H_B * D_B, MOBA_BLOCK).astype(BF16)
        return ls, acc + _dot_nt(p, vt)

    p0 = jnp.exp(s0 - mb)
    ls, acc = lax.fori_loop(0, n_blocks, attend_chunk, (p0, _dot_nt(p0.astype(BF16), vn_t)), unroll=min(8, n_blocks))
    ob_ref[0] = _diag_heads(acc / jnp.sum(ls, axis=1, keepdims=True), n_tok).astype(ob_ref.dtype)


def _moba_sample(page_table, q3, kn3, vn3, qm3, mem_kt, mem_vt, cache_kt, cache_vt, layer):
    ns, n_pages = page_table.shape
    page = cache_kt.shape[-1]
    n_tok = q3.shape[1]
    t_past = n_pages * page
    w = H_B * D_B
    mem_len = mem_kt.shape[-1]
    blk = pl.BlockSpec((1, n_tok, w), lambda s, pt: (s, 0, 0))
    mem_blk = pl.BlockSpec((1, 1, H_M, D_M, mem_len), lambda s, pt: (layer, s, 0, 0, 0))
    return pl.pallas_call(
        functools.partial(_moba_sample_kernel, layer=layer, n_pages=n_pages, page=page, n_tok=n_tok),
        grid_spec=pltpu.PrefetchScalarGridSpec(
            num_scalar_prefetch=1, grid=(ns,),
            in_specs=[blk, blk, blk, blk, mem_blk, mem_blk,
                      pl.BlockSpec(memory_space=pl.ANY), pl.BlockSpec(memory_space=pl.ANY)],
            out_specs=[blk, blk],
            scratch_shapes=[pltpu.VMEM((2, H_B, D_B, t_past), F32), pltpu.VMEM((2, H_B, D_B, t_past), F32),
                            pltpu.VMEM((n_tok * H_B, t_past), F32), pltpu.SemaphoreType.DMA((2, 2))]),
        out_shape=[jax.ShapeDtypeStruct((ns, n_tok, w), BF16)] * 2,
        compiler_params=_cparams(1), name="moba_sample",
    )(page_table, q3, kn3, vn3, qm3, mem_kt, mem_vt, cache_kt, cache_vt)


def _outproj_kernel(*refs, alpha, latent):
    if latent:
        x_ref, oa_ref, ob_ref, om_ref, wuv_ref, wo_ref, g_ref, b_ref, y_ref = refs
        oa = _dot(oa_ref[...], wuv_ref[...]).astype(BF16)
    else:
        x_ref, oa_ref, ob_ref, om_ref, wo_ref, g_ref, b_ref, y_ref = refs
        oa = oa_ref[...]
    wa, wb = H_A * V_A, H_A * V_A + H_B * D_B
    att = _dot(oa, wo_ref[:wa, :]) + _dot(ob_ref[...], wo_ref[wa:wb, :]) + _dot(om_ref[...], wo_ref[wb:, :])
    y_ref[...] = _layer_norm(alpha * x_ref[...] + att, g_ref[...], b_ref[...])


def _outproj(x2, oa, ob, om, w, alpha, wuv=None):
    n, d = x2.shape
    tm = min(512, n)
    row = lambda wd: pl.BlockSpec((tm, wd), lambda i: (i, 0))
    in_specs = [row(d), row(oa.shape[1]), row(ob.shape[1]), row(om.shape[1])]
    args = [x2, oa, ob, om]
    if wuv is not None:
        in_specs.append(_const_spec(wuv.shape))
        args.append(wuv)
    in_specs += [_const_spec(w["wo"].shape), _const_spec((1, d)), _const_spec((1, d))]
    args += [w["wo"], w["ln1_g"], w["ln1_b"]]
    return pl.pallas_call(
        functools.partial(_outproj_kernel, alpha=alpha, latent=wuv is not None),
        grid=(n // tm,), in_specs=in_specs, out_specs=row(d),
        out_shape=jax.ShapeDtypeStruct((n, d), F32),
        compiler_params=_cparams(1), name="outproj",
    )(*args)


def _ffn_kernel(*refs, alpha, tm, tf, tiles_per_seq, sample, n_tok):
    if sample:
        x_ref, s1_ref, s2_ref, wg_ref, wv_ref, wd_ref, cw_ref, cb_ref, g_ref, b_ref, y_ref, gate_ref, gbuf = refs
    else:
        x_ref, wg_ref, wv_ref, wd_ref, cw_ref, cb_ref, g_ref, b_ref, y_ref, st_ref, gbuf, carry_ref = refs
        first = (pl.program_id(0) % tiles_per_seq) == 0
    x = x_ref[...]
    xb = x.astype(BF16)
    acc = jnp.zeros(x.shape, F32)
    if sample:
        tpos = _iota((tm, tf), 0) % n_tok
        gbuf[0:SUBLANES, :] = jnp.zeros((SUBLANES, tf), F32)
    for c in range(D_FF // tf):
        cs = slice(c * tf, (c + 1) * tf)
        gate = _dot(xb, wg_ref[:, cs])
        val = _dot(xb, wv_ref[:, cs])
        gbuf[SUBLANES:SUBLANES + tm, :] = gate
        if sample:
            gate_ref[:, cs] = gate
            g1 = jnp.where(tpos >= 1, gbuf[SUBLANES - 1:SUBLANES - 1 + tm, :], s1_ref[:, cs])
            g2 = jnp.where(tpos >= 2, gbuf[SUBLANES - 2:SUBLANES - 2 + tm, :], s2_ref[:, cs])
        else:
            @pl.when(first)
            def _():
                gbuf[0:SUBLANES, :] = jnp.zeros((SUBLANES, tf), F32)

            @pl.when(jnp.logical_not(first))
            def _():
                gbuf[0:SUBLANES, :] = carry_ref[:, cs]

            g1 = gbuf[SUBLANES - 1:SUBLANES - 1 + tm, :]
            g2 = gbuf[SUBLANES - 2:SUBLANES - 2 + tm, :]
            carry_ref[:, cs] = gate[tm - SUBLANES:, :]
            st_ref[0, :, cs] = gate[tm - (CONV_W - 1):, :]
        pre = cb_ref[:, cs] + cw_ref[0:1, cs] * g2 + cw_ref[1:2, cs] * g1 + cw_ref[2:3, cs] * gate
        hid = (_gelu_tanh(pre) * val).astype(BF16)
        acc = acc + _dot(hid, wd_ref[cs, :])
    y_ref[...] = _layer_norm(alpha * x + acc, g_ref[...], b_ref[...])


def _ffn(x2, w, alpha, seq_len, prev=None):
    n, d = x2.shape
    tm = min(256, n) if prev is not None else min(512, seq_len)
    tf = 256
    row = lambda wd: pl.BlockSpec((tm, wd), lambda i: (i, 0))
    wspecs = [_const_spec(w["wg"].shape), _const_spec(w["wv_up"].shape), _const_spec(w["wd"].shape),
              _const_spec((CONV_W, D_FF)), _const_spec((1, D_FF)), _const_spec((1, d)), _const_spec((1, d))]
    wargs = [w["wg"], w["wv_up"], w["wd"], w["cw"], w["cb"], w["ln2_g"], w["ln2_b"]]
    if prev is None:
        tiles_per_seq = seq_len // tm
        n_seq = n // seq_len
        kern = functools.partial(_ffn_kernel, alpha=alpha, tm=tm, tf=tf, tiles_per_seq=tiles_per_seq, sample=False, n_tok=0)
        return pl.pallas_call(
            kern, grid=(n // tm,), in_specs=[row(d)] + wspecs,
            out_specs=[row(d), pl.BlockSpec((1, CONV_W - 1, D_FF), lambda i: (i // tiles_per_seq, 0, 0))],
            out_shape=[jax.ShapeDtypeStruct((n, d), F32), jax.ShapeDtypeStruct((n_seq, CONV_W - 1, D_FF), F32)],
            scratch_shapes=[pltpu.VMEM((tm + SUBLANES, tf), F32), pltpu.VMEM((SUBLANES, D_FF), F32)],
            compiler_params=_cparams(1), name="ffn_prompt",
        )(x2, *wargs)
    s1, s2 = prev
    kern = functools.partial(_ffn_kernel, alpha=alpha, tm=tm, tf=tf, tiles_per_seq=1, sample=True, n_tok=seq_len)
    return pl.pallas_call(
        kern, grid=(n // tm,), in_specs=[row(d), row(D_FF), row(D_FF)] + wspecs,
        out_specs=[row(d), row(D_FF)],
        out_shape=[jax.ShapeDtypeStruct((n, d), F32), jax.ShapeDtypeStruct((n, D_FF), F32)],
        scratch_shapes=[pltpu.VMEM((tm + SUBLANES, tf), F32)],
        compiler_params=_cparams(1), name="ffn_sample",
    )(x2, s1, s2, *wargs)


def _pack_layer(l, w_in, mla_q_norm, mla_kv_norm, w_uq, w_uk, w_uv, w_mem_kv, w_out, ln1_g, ln1_b, w_up, conv_w,
                conv_b, w_down, ln2_g, ln2_b):
    wi = w_in[l]
    d = wi.shape[0]
    o = [0, Q_LORA, Q_LORA + KV_LORA, Q_LORA + KV_LORA + QK_ROPE]
    o += [o[3] + H_B * D_B, o[3] + 2 * H_B * D_B, o[3] + 3 * H_B * D_B]
    cq, ckv, kpe = wi[:, :o[1]], wi[:, o[1]:o[2]], wi[:, o[2]:o[3]]
    qb, kb, vb, qm = wi[:, o[3]:o[4]], wi[:, o[4]:o[5]], wi[:, o[5]:o[6]], wi[:, o[6]:]
    k1, k2 = kpe[:, :ROPE_HALF], kpe[:, ROPE_HALF:]
    zpad = jnp.zeros((d, LANES - QK_ROPE), wi.dtype)
    win = jnp.concatenate([cq, ckv, qb, kb, vb, qm, k1, k2, zpad, k2, k1, zpad], axis=1).astype(BF16)
    uq = w_uq[l].reshape(Q_LORA, H_A, QK_NOPE + QK_ROPE)
    nope, p1, p2 = uq[..., :QK_NOPE], uq[..., QK_NOPE:QK_NOPE + ROPE_HALF], uq[..., QK_NOPE + ROPE_HALF:]
    z32 = jnp.zeros((Q_LORA, H_A, 64 - QK_ROPE), uq.dtype)
    z96 = jnp.zeros((Q_LORA, H_A, LANES - QK_ROPE), uq.dtype)
    wa = jnp.concatenate([p1, p2, z32, nope], axis=2).reshape(Q_LORA, H_A * LANES).astype(BF16)
    wb = jnp.concatenate([p2, p1, z96], axis=2).reshape(Q_LORA, H_A * LANES).astype(BF16)
    uk, uv = w_uk[l], w_uv[l]
    wk = jnp.concatenate([jnp.zeros_like(uk), uk], axis=2).reshape(KV_LORA, H_A * LANES).astype(BF16)
    wv = uv.reshape(KV_LORA, H_A * V_A).astype(BF16)
    uk_t = jnp.transpose(uk, (1, 2, 0))
    wql = jnp.concatenate([jnp.zeros_like(uk_t), uk_t], axis=1).astype(BF16)
    eye = jnp.eye(H_A, dtype=uv.dtype)
    wuv_bd = (eye[:, None, :, None] * jnp.transpose(uv, (1, 0, 2))[:, :, None, :]).reshape(H_A * KV_LORA, H_A * V_A).astype(BF16)
    up = w_up[l]
    return dict(
        win=win, qn=mla_q_norm[l][None], kvn=mla_kv_norm[l][None], wa=wa, wb=wb, wk=wk, wv=wv, wql=wql, wuv_bd=wuv_bd,
        wmem=w_mem_kv[l].astype(BF16), wo=w_out[l].astype(BF16), ln1_g=ln1_g[l][None], ln1_b=ln1_b[l][None],
        wg=up[:, :D_FF].astype(BF16), wv_up=up[:, D_FF:].astype(BF16), wd=w_down[l].astype(BF16),
        cw=conv_w[l], cb=conv_b[l][None], ln2_g=ln2_g[l][None], ln2_b=ln2_b[l][None])


def _rope_tables(pos):
    inv = ROPE_THETA ** (-jnp.arange(0, QK_ROPE, 2, dtype=F32) / QK_ROPE)
    ang = pos.astype(F32)[:, None] * inv
    cos, sin = jnp.cos(ang), jnp.sin(ang)
    n = pos.shape[0]
    ones = jnp.ones((n, LANES - QK_ROPE), F32)
    return (jnp.concatenate([cos, cos, ones], axis=1), jnp.concatenate([-sin, sin, 0.0 * ones], axis=1))


def kernel(x_prompt, x_sample, cache_mla_ckv, cache_mla_kpe, cache_moba_k, cache_moba_v, cache_mem_k, cache_mem_v, state_conv, page_table, mem_prompt, w_in, mla_q_norm, mla_kv_norm, w_uq, w_uk, w_uv, w_mem_kv, w_out, ln1_g, ln1_b, w_up, conv_w, conv_b, w_down, ln2_g, ln2_b):
    b, s, d = x_prompt.shape
    ns, t_s, _ = x_sample.shape
    depth = w_in.shape[0]
    mem_len = mem_prompt.shape[1]
    n_pages, page = page_table.shape[1], cache_mla_ckv.shape[2]
    past = n_pages * page
    assert s % MOBA_BLOCK == 0 and s // MOBA_BLOCK <= MAX_MOBA_BLOCKS and past % MOBA_BLOCK == 0
    assert t_s == SUBLANES and past // MOBA_BLOCK <= LANES
    alpha = (2 * depth) ** 0.25

    cache_kpt = jnp.transpose(cache_mla_kpe, (0, 1, 3, 2))
    cache_kt = jnp.transpose(cache_moba_k, (0, 1, 3, 4, 2))
    cache_vt = jnp.transpose(cache_moba_v, (0, 1, 3, 4, 2))
    mem_kt = jnp.transpose(cache_mem_k, (0, 1, 3, 4, 2))
    mem_vt = jnp.transpose(cache_mem_v, (0, 1, 3, 4, 2))

    cos_p, sin_p = _rope_tables(jnp.arange(s))
    tm_s = min(512, ns * t_s)
    cos_s, sin_s = _rope_tables(jnp.tile(past + jnp.arange(t_s), tm_s // t_s))
    blk_of = jnp.arange(s) // MOBA_BLOCK
    avg = jnp.where((jnp.arange(LANES)[:, None] - ONEHOT0) == blk_of[None, :], 1.0 / MOBA_BLOCK, 0.0).astype(BF16)

    hp = x_prompt.reshape(b * s, d)
    hs = x_sample.reshape(ns * t_s, d)
    mem2 = mem_prompt.reshape(b * mem_len, d)
    outs = {k: [] for k in ("p_ckv", "p_kpe", "p_k", "p_v", "p_mk", "p_mv", "p_cv", "s_ckv", "s_kpe", "s_k", "s_v", "s_cv")}
    for l in range(depth):
        w = _pack_layer(l, w_in, mla_q_norm, mla_kv_norm, w_uq, w_uk, w_uv, w_mem_kv, w_out, ln1_g, ln1_b, w_up,
                        conv_w, conv_b, w_down, ln2_g, ln2_b)
        ckv, kpe, kb, vb, qa, ka, va, qbp, kbp, vbb, qmp = _proj_prompt(hp, w, cos_p, sin_p, s)
        mk, mv, mkp, mvb = _memkv(mem2, w["wmem"])
        r3 = lambda a: a.reshape(b, -1, a.shape[-1])
        o_a = _flash(r3(qa), r3(ka), r3(va), "causal")
        o_b = _flash(r3(qbp), r3(kbp), r3(vbb), "moba", avg)
        o_m = _flash(r3(qmp), r3(mkp), r3(mvb), "full")
        f2 = lambda a: a.reshape(b * s, a.shape[-1])
        hp = _outproj(hp, f2(o_a), f2(o_b), f2(o_m), w, alpha)
        hp, cv = _ffn(hp, w, alpha, s)
        outs["p_ckv"].append(ckv.reshape(b, s, KV_LORA))
        outs["p_kpe"].append(kpe.reshape(b, s, QK_ROPE))
        outs["p_k"].append(kb.reshape(b, s, H_B, D_B))
        outs["p_v"].append(vb.reshape(b, s, H_B, D_B))
        outs["p_mk"].append(mk.reshape(b, mem_len, H_M, D_M))
        outs["p_mv"].append(mv.reshape(b, mem_len, H_M, D_M))
        outs["p_cv"].append(cv)
        ckv, kpe, kb, vb, qa, qlat, qb, qm = _proj_sample(hs, w, cos_s, sin_s)
        o_lat = _mla_sample(page_table, qlat.reshape(ns, t_s * H_A, KV_LORA), qa.reshape(ns, t_s * H_A, LANES),
                            ckv.reshape(ns, t_s, KV_LORA), kpe.reshape(ns, t_s, QK_ROPE), cache_mla_ckv, cache_kpt, l)
        t3 = lambda a: a.reshape(ns, t_s, a.shape[-1])
        o_b, o_m = _moba_sample(page_table, t3(qb), t3(kb), t3(vb), t3(qm), mem_kt, mem_vt, cache_kt, cache_vt, l)
        hs = _outproj(hs, o_lat.reshape(ns * t_s, H_A * KV_LORA), o_b.reshape(ns * t_s, -1), o_m.reshape(ns * t_s, -1),
                      w, alpha, wuv=w["wuv_bd"])
        prev = state_conv[l]
        zrow = jnp.zeros((ns, t_s - 1, D_FF), F32)
        s1 = jnp.concatenate([prev[:, 1:2], zrow], axis=1).reshape(ns * t_s, D_FF)
        s2 = jnp.concatenate([prev, zrow[:, 1:]], axis=1).reshape(ns * t_s, D_FF)
        hs, gate = _ffn(hs, w, alpha, t_s, prev=(s1, s2))
        outs["s_ckv"].append(ckv.reshape(ns, t_s, KV_LORA))
        outs["s_kpe"].append(kpe.reshape(ns, t_s, QK_ROPE))
        outs["s_k"].append(kb.reshape(ns, t_s, H_B, D_B))
        outs["s_v"].append(vb.reshape(ns, t_s, H_B, D_B))
        outs["s_cv"].append(gate.reshape(ns, t_s, D_FF)[:, t_s - (CONV_W - 1):])
    st = lambda k: jnp.stack(outs[k])
    return (hp.reshape(b, s, d), hs.reshape(ns, t_s, d),
            st("p_ckv"), st("p_kpe"), st("p_k"), st("p_v"), st("p_mk"), st("p_mv"), st("p_cv"),
            st("s_ckv"), st("s_kpe"), st("s_k"), st("s_v"), st("s_cv"))
```

```python
import functools

import jax
import jax.numpy as jnp
from jax import lax
from jax.experimental import pallas as pl
from jax.experimental.pallas import tpu as pltpu

F32 = jnp.float32
BF16 = jnp.bfloat16

H_A, QK_NOPE, QK_ROPE, V_A = 8, 64, 32, 64
Q_LORA, KV_LORA = 384, 256
H_B, D_B, MOBA_BLOCK, MOBA_TOPK = 4, 64, 256, 3
H_M, D_M = 4, 64
D_FF, CONV_W = 2816, 3
ROPE_THETA = 10000.0
NORM_EPS = 1e-5
RMS_EPS = 1e-6
NEG = -1e30

LANES = 128
SUBLANES = 8
VMEM_LIMIT = 56 * 1024 * 1024

ROPE_HALF = QK_ROPE // 2
AUG0 = 64
ONEHOT0 = 80
MAX_MOBA_BLOCKS = 16

C_CQ, C_CKV, C_QB, C_KB, C_VB, C_QM, C_KPA, C_KPB, C_END = 0, 384, 640, 896, 1152, 1408, 1664, 1792, 1920


def _cparams(n_axes):
    return pltpu.CompilerParams(dimension_semantics=("arbitrary",) * n_axes, vmem_limit_bytes=VMEM_LIMIT)


def _const_spec(shape):
    nd = len(shape)
    return pl.BlockSpec(shape, lambda *_: (0,) * nd, pipeline_mode=pl.Buffered(1))


def _dot(a, b):
    return jnp.dot(a, b, preferred_element_type=F32)


def _dot_nt(a, b):
    return lax.dot_general(a, b, (((1,), (1,)), ((), ())), preferred_element_type=F32)


def _iota(shape, dim):
    return lax.broadcasted_iota(jnp.int32, shape, dim)


def _rms(x, g):
    return x * lax.rsqrt(jnp.mean(x * x, axis=-1, keepdims=True) + RMS_EPS) * g


def _layer_norm(x, g, b):
    mu = jnp.mean(x, axis=-1, keepdims=True)
    xc = x - mu
    var = jnp.mean(xc * xc, axis=-1, keepdims=True)
    return xc * lax.rsqrt(var + NORM_EPS) * g + b


def _gelu_tanh(x):
    return x * (0.5 * (1.0 + jnp.tanh(0.7978845608028654 * (x + 0.044715 * (x * x * x)))))


def _head_to_lanes(seg, h):
    blk = seg[:, (h // 2) * LANES:(h // 2 + 1) * LANES]
    if h % 2:
        blk = pltpu.roll(blk, 64, axis=1)
    return jnp.where(_iota(blk.shape, 1) < 64, blk, 0.0)


def _ones_hi(m):
    return jnp.where(_iota((m, LANES), 1) >= 64, 1.0, 0.0)


def _slope(h):
    return 2.0 ** (-8.0 * (h + 1) / H_B)


def _proj_common(x_ref, win_ref, qn_ref, kvn_ref, wa_ref, wb_ref, c_ref, s_ref, ckv_ref, kpe_ref, kb_ref, vb_ref, qa_ref):
    xb = x_ref[...].astype(BF16)

    def seg(a, b):
        return _dot(xb, win_ref[:, a:b])

    cos_t, sin_t = c_ref[...], s_ref[...]
    scale_a = (QK_NOPE + QK_ROPE) ** -0.5
    cqn = _rms(seg(C_CQ, C_CKV), qn_ref[...]).astype(BF16)
    for h in range(H_A):
        hs = slice(h * LANES, (h + 1) * LANES)
        q_rot = _dot(cqn, wa_ref[:, hs]) * cos_t + _dot(cqn, wb_ref[:, hs]) * sin_t
        qa_ref[:, hs] = (q_rot * scale_a).astype(BF16)
    ckvn = _rms(seg(C_CKV, C_QB), kvn_ref[...])
    ckv_ref[...] = ckvn
    kpe_rot = seg(C_KPA, C_KPB) * cos_t + seg(C_KPB, C_END) * sin_t
    kpe_ref[...] = kpe_rot[:, :QK_ROPE]
    qb, kb, vb, qm = seg(C_QB, C_KB), seg(C_KB, C_VB), seg(C_VB, C_QM), seg(C_QM, C_KPA)
    kb_ref[...] = kb
    vb_ref[...] = vb
    return ckvn.astype(BF16), kpe_rot, qb, kb, vb, qm


def _proj_prompt_kernel(x_ref, win_ref, qn_ref, kvn_ref, wa_ref, wb_ref, wk_ref, wv_ref, c_ref, s_ref,
                        ckv_ref, kpe_ref, kb_ref, vb_ref, qa_ref, ka_ref, va_ref, qbp_ref, kbp_ref, vbb_ref, qmp_ref,
                        *, tm, tiles_per_seq):
    ckvb, kpe_rot, qb, kb, vb, qm = _proj_common(x_ref, win_ref, qn_ref, kvn_ref, wa_ref, wb_ref, c_ref, s_ref,
                                                 ckv_ref, kpe_ref, kb_ref, vb_ref, qa_ref)
    ones_hi = _ones_hi(tm)
    for h in range(H_A):
        hs = slice(h * LANES, (h + 1) * LANES)
        ka_ref[:, hs] = (_dot(ckvb, wk_ref[:, hs]) + kpe_rot).astype(BF16)
        va_ref[:, hs] = (_dot(ckvb, wv_ref[:, hs]) + ones_hi).astype(BF16)
    for h in range(H_B):
        vbb_ref[:, h * LANES:(h + 1) * LANES] = (_head_to_lanes(vb, h) + ones_hi).astype(BF16)
    pos = (pl.program_id(0) % tiles_per_seq) * tm + _iota((tm, LANES), 0)
    lane = _iota((tm, LANES), 1)
    blk = (pos >> 8).astype(F32)
    rem = (pos & (MOBA_BLOCK - 1)).astype(F32)
    k_aug = jnp.where(lane < AUG0 + 2, 1.0, jnp.where(lane == AUG0 + 2, blk, rem))
    k_aug = jnp.where((lane >= AUG0) & (lane < AUG0 + 4), k_aug, 0.0)
    k_aug = jnp.where((lane - ONEHOT0) == (pos >> 8), 1.0, k_aug)
    q_aug = jnp.where(lane == AUG0, -float(MOBA_BLOCK) * blk,
                      jnp.where(lane == AUG0 + 1, -rem, jnp.where(lane == AUG0 + 2, float(MOBA_BLOCK), 1.0)))
    q_aug = jnp.where((lane >= AUG0) & (lane < AUG0 + 4), q_aug, 0.0)
    for h in range(H_B):
        hs = slice(h * LANES, (h + 1) * LANES)
        qbp_ref[:, hs] = (_head_to_lanes(qb, h) * (D_B ** -0.5) + _slope(h) * q_aug).astype(BF16)
        kbp_ref[:, hs] = (_head_to_lanes(kb, h) + k_aug).astype(BF16)
        qmp_ref[:, hs] = (_head_to_lanes(qm, h) * (D_M ** -0.5)).astype(BF16)


def _proj_sample_kernel(x_ref, win_ref, qn_ref, kvn_ref, wa_ref, wb_ref, wql_ref, c_ref, s_ref,
                        ckv_ref, kpe_ref, kb_ref, vb_ref, qa_ref, qlat_ref, qb_ref, qm_ref):
    _, _, qb, _, _, qm = _proj_common(x_ref, win_ref, qn_ref, kvn_ref, wa_ref, wb_ref, c_ref, s_ref,
                                      ckv_ref, kpe_ref, kb_ref, vb_ref, qa_ref)
    for h in range(H_A):
        qlat_ref[:, h * KV_LORA:(h + 1) * KV_LORA] = _dot(qa_ref[:, h * LANES:(h + 1) * LANES], wql_ref[h]).astype(BF16)
    qb_ref[...] = qb * (D_B ** -0.5)
    qm_ref[...] = qm * (D_M ** -0.5)


def _proj_specs(tm, n_tab):
    row = lambda w: pl.BlockSpec((tm, w), lambda i: (i, 0))
    tab = pl.BlockSpec((tm, LANES), lambda i: (i % n_tab, 0))
    return row, tab


def _proj_prompt(x2, w, cos_t, sin_t, seq_len):
    n, d = x2.shape
    tm = min(512, seq_len)
    tiles_per_seq = seq_len // tm
    row, tab = _proj_specs(tm, tiles_per_seq)
    out_w = [(KV_LORA, F32), (QK_ROPE, F32), (H_B * D_B, F32), (H_B * D_B, F32), (H_A * LANES, BF16), (H_A * LANES, BF16),
             (H_A * LANES, BF16), (H_B * LANES, BF16), (H_B * LANES, BF16), (H_B * LANES, BF16), (H_M * LANES, BF16)]
    return pl.pallas_call(
        functools.partial(_proj_prompt_kernel, tm=tm, tiles_per_seq=tiles_per_seq),
        grid=(n // tm,),
        in_specs=[row(d), _const_spec(w["win"].shape), _const_spec((1, Q_LORA)), _const_spec((1, KV_LORA)),
                  _const_spec(w["wa"].shape), _const_spec(w["wb"].shape), _const_spec(w["wk"].shape),
                  _const_spec(w["wv"].shape), tab, tab],
        out_specs=[row(wd) for wd, _ in out_w],
        out_shape=[jax.ShapeDtypeStruct((n, wd), dt) for wd, dt in out_w],
        compiler_params=_cparams(1), name="proj_prompt",
    )(x2, w["win"], w["qn"], w["kvn"], w["wa"], w["wb"], w["wk"], w["wv"], cos_t, sin_t)


def _proj_sample(x2, w, cos_t, sin_t):
    n, d = x2.shape
    tm = min(512, n)
    row, tab = _proj_specs(tm, cos_t.shape[0] // tm)
    out_w = [(KV_LORA, F32), (QK_ROPE, F32), (H_B * D_B, F32), (H_B * D_B, F32), (H_A * LANES, BF16),
             (H_A * KV_LORA, BF16), (H_B * D_B, F32), (H_M * D_M, F32)]
    return pl.pallas_call(
        _proj_sample_kernel,
        grid=(n // tm,),
        in_specs=[row(d), _const_spec(w["win"].shape), _const_spec((1, Q_LORA)), _const_spec((1, KV_LORA)),
                  _const_spec(w["wa"].shape), _const_spec(w["wb"].shape), _const_spec(w["wql"].shape), tab, tab],
        out_specs=[row(wd) for wd, _ in out_w],
        out_shape=[jax.ShapeDtypeStruct((n, wd), dt) for wd, dt in out_w],
        compiler_params=_cparams(1), name="proj_sample",
    )(x2, w["win"], w["qn"], w["kvn"], w["wa"], w["wb"], w["wql"], cos_t, sin_t)


def _memkv_kernel(x_ref, w_ref, mk_ref, mv_ref, mkp_ref, mvb_ref):
    kv = _dot(x_ref[...].astype(BF16), w_ref[...])
    mk, mv = kv[:, :H_M * D_M], kv[:, H_M * D_M:]
    mk_ref[...] = mk
    mv_ref[...] = mv
    ones_hi = _ones_hi(mk.shape[0])
    for h in range(H_M):
        mkp_ref[:, h * LANES:(h + 1) * LANES] = _head_to_lanes(mk, h).astype(BF16)
        mvb_ref[:, h * LANES:(h + 1) * LANES] = (_head_to_lanes(mv, h) + ones_hi).astype(BF16)


def _memkv(mem2, w_bf):
    n, d = mem2.shape
    tm = min(256, n)
    row = lambda w: pl.BlockSpec((tm, w), lambda i: (i, 0))
    out_w = [(H_M * D_M, F32), (H_M * D_M, F32), (H_M * LANES, BF16), (H_M * LANES, BF16)]
    return pl.pallas_call(
        _memkv_kernel, grid=(n // tm,),
        in_specs=[row(d), _const_spec(w_bf.shape)],
        out_specs=[row(wd) for wd, _ in out_w],
        out_shape=[jax.ShapeDtypeStruct((n, wd), dt) for wd, dt in out_w],
        compiler_params=_cparams(1), name="mem_kv",
    )(mem2, w_bf)


def _moba_bias(q, km, qi):
    tq = q.shape[0]
    gate_t = _dot_nt(km, q)[ONEHOT0:ONEHOT0 + MAX_MOBA_BLOCKS, :]
    c = _iota(gate_t.shape, 0)
    valid = c < qi
    gm = jnp.where(valid, gate_t, -jnp.inf)
    ahead = jnp.zeros(gate_t.shape, F32)
    for cp in range(MAX_MOBA_BLOCKS):
        r = gm[cp:cp + 1, :]
        ahead = ahead + jnp.where(r > gm, 1.0, jnp.where(r == gm, jnp.where(cp < c, 1.0, 0.0), 0.0))
    keep = jnp.where(valid, jnp.where(ahead < MOBA_TOPK, 0.0, NEG), jnp.where(c == qi, 0.0, NEG))
    keep_t = jnp.concatenate([jnp.zeros((ONEHOT0, tq), F32), keep,
                              jnp.zeros((LANES - ONEHOT0 - MAX_MOBA_BLOCKS, tq), F32)], axis=0)
    bias = keep_t.T
    lane = _iota(bias.shape, 1)
    return jnp.where((lane >= ONEHOT0) & (lane < ONEHOT0 + MAX_MOBA_BLOCKS), bias.astype(q.dtype), q)


def _flash_kernel(*refs, tq, tk, mode, n_kblocks, nh):
    if mode == "moba":
        q_ref, k_ref, v_ref, avg_ref, o_ref, s_ref, mx_ref, acc_ref, km_ref = refs
    else:
        q_ref, k_ref, v_ref, o_ref, s_ref, mx_ref, acc_ref = refs
    qi = pl.program_id(2)
    pieces = tk // LANES
    hsl = [slice(hh * LANES, (hh + 1) * LANES) for hh in range(nh)]
    qs = []
    for hh in range(nh):
        q = q_ref[0, :, hsl[hh]]
        if mode == "moba":
            @pl.when(qi == 0)
            def _():
                km = _dot(avg_ref[...], k_ref[0, :, hsl[hh]])
                km_ref[hh] = jnp.where(_iota(km.shape, 1) < D_B, km, 0.0).astype(BF16)

            q = _moba_bias(q, km_ref[hh], qi)
        qs.append(q)

    def score(slot, j, first):
        src = pl.multiple_of(j * tk, tk)
        dst = pl.multiple_of(slot * tk, tk)
        for hh in range(nh):
            s = _dot_nt(qs[hh], k_ref[0, pl.ds(src, tk), hsl[hh]])
            if first and mode != "full":
                s = jnp.where(qi * tq + _iota(s.shape, 0) >= src + _iota(s.shape, 1), s, NEG)
            s_ref[hh, :, pl.ds(dst, tk)] = s
            f = _fold_lanes(s, jnp.maximum)
            mx_ref[hh] = f if first else jnp.maximum(mx_ref[hh], f)

    def attend(slot, j, first):
        src = pl.multiple_of(j * tk, tk)
        dst = pl.multiple_of(slot * tk, tk)
        for hh in range(nh):
            mb = mx_ref[hh]
            ps = [jnp.exp(s_ref[hh, :, pl.ds(dst + i * LANES, LANES)] - mb).astype(BF16) for i in range(pieces)]
            p = jnp.concatenate(ps, axis=1) if pieces > 1 else ps[0]
            pv = _dot(p, v_ref[0, pl.ds(src, tk), hsl[hh]])
            acc_ref[hh] = pv if first else acc_ref[hh] + pv

    def sweep(fn):
        if mode == "full":
            fn(0, 0, True)
            lax.fori_loop(1, n_kblocks, lambda t, c: (fn(t, t, False), c)[1], 0)
        else:
            last = (qi * tq) // tk
            fn(0, last, True)
            lax.fori_loop(1, last + 1, lambda t, c: (fn(t, t - 1, False), c)[1], 0)

    sweep(score)
    for hh in range(nh):
        mx_ref[hh] = jnp.broadcast_to(jnp.max(mx_ref[hh], axis=1, keepdims=True), (tq, LANES))
    sweep(attend)
    low = _iota((tq, LANES), 1) < 64
    for pp in range(nh // 2):
        a0, a1 = acc_ref[2 * pp], acc_ref[2 * pp + 1]
        o0 = a0 / pltpu.roll(a0, 64, axis=1)
        o1 = pltpu.roll(a1, 64, axis=1) / a1
        o_ref[0, :, pp * LANES:(pp + 1) * LANES] = jnp.where(low, o0, o1).astype(o_ref.dtype)


def _flash(q, k, v, mode, avg=None):
    b, s, w = q.shape
    sk = k.shape[1]
    n_heads = w // LANES
    nh = min(4, n_heads)
    tq = min(4 * MOBA_BLOCK, s) if mode == "full" else min(MOBA_BLOCK, s)
    tk = min(MOBA_BLOCK, sk) if mode == "full" else min(2 * MOBA_BLOCK, sk)
    assert sk % tk == 0 and n_heads % nh == 0 and nh % 2 == 0
    in_specs = [pl.BlockSpec((1, tq, nh * LANES), lambda bi, g, i: (bi, i, g)),
                pl.BlockSpec((1, sk, nh * LANES), lambda bi, g, i: (bi, 0, g)),
                pl.BlockSpec((1, sk, nh * LANES), lambda bi, g, i: (bi, 0, g))]
    args = [q, k, v]
    scratch = [pltpu.VMEM((nh, tq, sk), F32)] + [pltpu.VMEM((nh, tq, LANES), F32)] * 2
    if mode == "moba":
        in_specs.append(_const_spec(avg.shape))
        args.append(avg)
        scratch.append(pltpu.VMEM((nh, LANES, LANES), BF16))
    return pl.pallas_call(
        functools.partial(_flash_kernel, tq=tq, tk=tk, mode=mode, n_kblocks=sk // tk, nh=nh),
        grid=(b, n_heads // nh, s // tq),
        in_specs=in_specs,
        out_specs=pl.BlockSpec((1, tq, nh * 64), lambda bi, g, i: (bi, i, g)),
        out_shape=jax.ShapeDtypeStruct((b, s, n_heads * 64), BF16),
        scratch_shapes=scratch,
        compiler_params=_cparams(3), name="flash_" + mode,
    )(*args)


def _fold_lanes(x, op):
    out = x[:, :LANES]
    for i in range(1, x.shape[1] // LANES):
        out = op(out, x[:, i * LANES:(i + 1) * LANES])
    return out


def _mla_sample_kernel(pt_ref, qlat_ref, qpe_ref, cn_ref, kn_ref, ckv_hbm, kpt_hbm, o_ref,
                       ckv_buf, kpt_buf, kc_ref, s_ref, sem, *, layer, n_pages, page, tk, n_tok):
    s = pl.program_id(0)
    ns = pl.num_programs(0)
    slot = s % 2
    rows = n_tok * H_A
    n_chunks = (n_pages * page) // tk
    pages_per_chunk = tk // page

    def copies(pg, sl, p):
        off = pl.multiple_of(p * page, page)
        return (pltpu.make_async_copy(ckv_hbm.at[layer, pg], ckv_buf.at[sl, pl.ds(off, page), :], sem.at[sl, 0]),
                pltpu.make_async_copy(kpt_hbm.at[layer, pg], kpt_buf.at[sl, p], sem.at[sl, 1]))

    def start_pages(seq, sl, p0, n):
        for i in range(n):
            for c in copies(pt_ref[seq, p0 + i], sl, p0 + i):
                c.start()

    @pl.when(s == 0)
    def _():
        lax.fori_loop(0, n_chunks, lambda c, x: (start_pages(0, 0, c * pages_per_chunk, pages_per_chunk), x)[1], 0)

    for p in range(n_pages):
        for c in copies(0, slot, p):
            c.wait()

    ql = qlat_ref[0]
    qp = qpe_ref[0][:, :QK_ROPE]

    def score_chunk(c, mx):
        @pl.when(s + 1 < ns)
        def _():
            start_pages(s + 1, 1 - slot, c * pages_per_chunk, pages_per_chunk)

        off = pl.multiple_of(c * tk, tk)
        kc = ckv_buf[slot, pl.ds(off, tk), :].astype(BF16)
        kc_ref[pl.ds(off, tk), :] = kc
        kp = jnp.concatenate([kpt_buf[slot, c * pages_per_chunk + i].astype(BF16) for i in range(pages_per_chunk)], axis=1)
        sc = _dot_nt(ql, kc) + _dot(qp, kp)
        s_ref[:, pl.ds(off, tk)] = sc
        return jnp.maximum(mx, _fold_lanes(sc, jnp.maximum))

    pad = jnp.zeros((LANES - n_tok, KV_LORA), F32)
    cn = jnp.concatenate([cn_ref[0], pad], axis=0).astype(BF16)
    kn = jnp.concatenate([kn_ref[0], pad[:, :QK_ROPE]], axis=0).astype(BF16)
    s0 = _dot_nt(ql, cn) + _dot_nt(qp, kn)
    s0 = jnp.where(_iota(s0.shape, 1) <= _iota(s0.shape, 0) // H_A, s0, NEG)
    mx = lax.fori_loop(0, n_chunks, score_chunk, s0, unroll=min(8, n_chunks))
    mb = jnp.broadcast_to(jnp.max(mx, axis=1, keepdims=True), (rows, LANES))

    def attend_chunk(c, carry):
        ls, acc = carry
        off = pl.multiple_of(c * tk, tk)
        ps = [jnp.exp(s_ref[:, pl.ds(off + i * LANES, LANES)] - mb) for i in range(tk // LANES)]
        for x in ps:
            ls = ls + x
        p = jnp.concatenate([x.astype(BF16) for x in ps], axis=1)
        return ls, acc + _dot(p, kc_ref[pl.ds(off, tk), :])

    p0 = jnp.exp(s0 - mb)
    ls, acc = lax.fori_loop(0, n_chunks, attend_chunk, (p0, _dot(p0.astype(BF16), cn)), unroll=min(8, n_chunks))
    o_ref[0] = (acc / jnp.sum(ls, axis=1, keepdims=True)).astype(o_ref.dtype)


def _mla_sample(page_table, qlat3, qpe3, ckv_new3, kpe_new3, cache_ckv, cache_kpt, layer):
    ns, n_pages = page_table.shape
    page = cache_ckv.shape[2]
    n_tok = ckv_new3.shape[1]
    rows = n_tok * H_A
    t_past = n_pages * page
    tk = min(512, t_past)
    blk = lambda r, w: pl.BlockSpec((1, r, w), lambda s, pt: (s, 0, 0))
    return pl.pallas_call(
        functools.partial(_mla_sample_kernel, layer=layer, n_pages=n_pages, page=page, tk=tk, n_tok=n_tok),
        grid_spec=pltpu.PrefetchScalarGridSpec(
            num_scalar_prefetch=1, grid=(ns,),
            in_specs=[blk(rows, KV_LORA), blk(rows, LANES), blk(n_tok, KV_LORA), blk(n_tok, QK_ROPE),
                      pl.BlockSpec(memory_space=pl.ANY), pl.BlockSpec(memory_space=pl.ANY)],
            out_specs=blk(rows, KV_LORA),
            scratch_shapes=[pltpu.VMEM((2, t_past, KV_LORA), F32), pltpu.VMEM((2, n_pages, QK_ROPE, page), F32),
                            pltpu.VMEM((t_past, KV_LORA), BF16), pltpu.VMEM((rows, t_past), F32),
                            pltpu.SemaphoreType.DMA((2, 2))]),
        out_shape=jax.ShapeDtypeStruct((ns, rows, KV_LORA), BF16),
        compiler_params=_cparams(1), name="mla_sample",
    )(page_table, qlat3, qpe3, ckv_new3, kpe_new3, cache_ckv, cache_kpt)


def _block_diag_rows(x8):
    head = _iota(x8.shape, 1) // D_B
    return jnp.concatenate([jnp.where(head == h, x8, 0.0) for h in range(H_B)], axis=0)


def _diag_heads(acc, n_tok):
    head = _iota((n_tok, acc.shape[1]), 1) // D_B
    out = jnp.zeros((n_tok, acc.shape[1]), F32)
    for h in range(H_B):
        out = out + jnp.where(head == h, acc[h * n_tok:(h + 1) * n_tok, :], 0.0)
    return out


def _moba_sample_kernel(pt_ref, q_ref, kn_ref, vn_ref, qm_ref, mk_ref, mv_ref, kt_hbm, vt_hbm, ob_ref, om_ref,
                        kt_buf, vt_buf, s_buf, sem, *, layer, n_pages, page, n_tok):
    s = pl.program_id(0)
    ns = pl.num_programs(0)
    slot = s % 2
    rows = n_tok * H_B
    t_past = n_pages * page
    n_blocks = t_past // MOBA_BLOCK

    pages_per_block = MOBA_BLOCK // page

    def copies(pg, sl, p):
        return (pltpu.make_async_copy(kt_hbm.at[layer, pg], kt_buf.at[sl, p], sem.at[sl, 0]),
                pltpu.make_async_copy(vt_hbm.at[layer, pg], vt_buf.at[sl, p], sem.at[sl, 1]))

    def block_t(buf, c):
        parts = [buf[slot, c * pages_per_block + i].reshape(H_B * D_B, page).astype(BF16) for i in range(pages_per_block)]
        return jnp.concatenate(parts, axis=1) if len(parts) > 1 else parts[0]

    def start_pages(seq, sl, p0, n):
        for i in range(n):
            for c in copies(pt_ref[seq, p0 + i], sl, p0 + i):
                c.start()

    @pl.when(s == 0)
    def _():
        lax.fori_loop(0, n_blocks, lambda c, x: (start_pages(0, 0, c * pages_per_block, pages_per_block), x)[1], 0)

    qmb = _block_diag_rows(qm_ref[0]).astype(BF16)
    hd = H_M * D_M
    sm = _dot(qmb, mk_ref[0, 0].reshape(hd, -1).astype(BF16))
    pm = jnp.exp(sm - jnp.max(sm, axis=1, keepdims=True))
    om = _dot_nt(pm.astype(BF16), mv_ref[0, 0].reshape(hd, -1).astype(BF16)) / jnp.sum(pm, axis=1, keepdims=True)
    om_ref[0] = _diag_heads(om, n_tok).astype(om_ref.dtype)

    for p in range(n_pages):
        for c in copies(0, slot, p):
            c.wait()

    qbd = _block_diag_rows(q_ref[0]).astype(BF16)
    row = _iota((rows, 1), 0)
    r_head, r_tok = row // n_tok, row % n_tok
    slope = jnp.where(r_head == 0, _slope(0), jnp.where(r_head == 1, _slope(1), jnp.where(r_head == 2, _slope(2), _slope(3))))
    lane = _iota((rows, LANES), 1)

    def score_chunk(c, gates):
        @pl.when(s + 1 < ns)
        def _():
            start_pages(s + 1, 1 - slot, c * pages_per_block, pages_per_block)

        off = pl.multiple_of(c * MOBA_BLOCK, MOBA_BLOCK)
        sr = _dot(qbd, block_t(kt_buf, c))
        s_buf[:, pl.ds(off, MOBA_BLOCK)] = sr
        return jnp.where(lane == c, jnp.sum(_fold_lanes(sr, jnp.add), axis=1, keepdims=True), gates)

    gates = lax.fori_loop(0, n_blocks, score_chunk, jnp.zeros((rows, LANES), F32), unroll=min(8, n_blocks))
    gm = jnp.where(lane < n_blocks, gates, -jnp.inf)
    ahead = jnp.zeros((rows, LANES), F32)
    for cp in range(n_blocks):
        col = gm[:, cp:cp + 1]
        ahead = ahead + jnp.where(col > gm, 1.0, jnp.where(col == gm, jnp.where(cp < lane, 1.0, 0.0), 0.0))
    sel_bias = jnp.where((lane < n_blocks) & (ahead < MOBA_TOPK), 0.0, NEG)

    pad = jnp.zeros((LANES - n_tok, H_B * D_B), F32)
    kn = jnp.concatenate([kn_ref[0], pad], axis=0).astype(BF16)
    vn_t = jnp.concatenate([vn_ref[0], pad], axis=0).T.astype(BF16)
    s0 = _dot_nt(qbd, kn) - slope * (r_tok - lane).astype(F32)
    s0 = jnp.where(lane <= r_tok, s0, NEG)

    alibi0 = slope * (_iota((rows, MOBA_BLOCK), 1) - r_tok - t_past).astype(F32)
    mx = s0
    for c in range(n_blocks):
        cs = slice(c * MOBA_BLOCK, (c + 1) * MOBA_BLOCK)
        sb = s_buf[:, cs] + alibi0 + (sel_bias[:, c:c + 1] + slope * float(c * MOBA_BLOCK))
        s_buf[:, cs] = sb
        mx = jnp.maximum(mx, _fold_lanes(sb, jnp.maximum))
    mb = jnp.broadcast_to(jnp.max(mx, axis=1, keepdims=True), (rows, LANES))

    def attend_chunk(c, carry):
        ls, acc = carry
        off = pl.multiple_of(c * MOBA_BLOCK, MOBA_BLOCK)
        ps = [jnp.exp(s_buf[:, pl.ds(off + i * LANES, LANES)] - mb) for i in range(MOBA_BLOCK // LANES)]
        for x in ps:
            ls = ls + x
        p = jnp.concatenate([x.astype(BF16) for x in ps], axis=1)
        return ls, acc + _dot_nt(p, block_t(vt_buf, c))

    p0 = jnp.exp(s0 - mb)
    ls, acc = lax.fori_loop(0, n_blocks, attend_chunk, (p0, _dot_nt(p0.astype(BF16), vn_t)), unroll=min(8, n_blocks))
    ob_ref[0] = _diag_heads(acc / jnp.sum(ls, axis=1, keepdims=True), n_tok).astype(ob_ref.dtype)


def _moba_sample(page_table, q3, kn3, vn3, qm3, mem_kt, mem_vt, cache_kt, cache_vt, layer):
    ns, n_pages = page_table.shape
    page = cache_kt.shape[-1]
    n_tok = q3.shape[1]
    t_past = n_pages * page
    w = H_B * D_B
    mem_len = mem_kt.shape[-1]
    blk = pl.BlockSpec((1, n_tok, w), lambda s, pt: (s, 0, 0))
    mem_blk = pl.BlockSpec((1, 1, H_M, D_M, mem_len), lambda s, pt: (layer, s, 0, 0, 0))
    return pl.pallas_call(
        functools.partial(_moba_sample_kernel, layer=layer, n_pages=n_pages, page=page, n_tok=n_tok),
        grid_spec=pltpu.PrefetchScalarGridSpec(
            num_scalar_prefetch=1, grid=(ns,),
            in_specs=[blk, blk, blk, blk, mem_blk, mem_blk,
                      pl.BlockSpec(memory_space=pl.ANY), pl.BlockSpec(memory_space=pl.ANY)],
            out_specs=[blk, blk],
            scratch_shapes=[pltpu.VMEM((2, n_pages, H_B, D_B, page), F32), pltpu.VMEM((2, n_pages, H_B, D_B, page), F32),
                            pltpu.VMEM((n_tok * H_B, t_past), F32), pltpu.SemaphoreType.DMA((2, 2))]),
        out_shape=[jax.ShapeDtypeStruct((ns, n_tok, w), BF16)] * 2,
        compiler_params=_cparams(1), name="moba_sample",
    )(page_table, q3, kn3, vn3, qm3, mem_kt, mem_vt, cache_kt, cache_vt)


def _outproj_kernel(*refs, alpha, latent):
    if latent:
        x_ref, oa_ref, ob_ref, om_ref, wuv_ref, wo_ref, g_ref, b_ref, y_ref = refs
        oa = _dot(oa_ref[...], wuv_ref[...]).astype(BF16)
    else:
        x_ref, oa_ref, ob_ref, om_ref, wo_ref, g_ref, b_ref, y_ref = refs
        oa = oa_ref[...]
    wa, wb = H_A * V_A, H_A * V_A + H_B * D_B
    att = _dot(oa, wo_ref[:wa, :]) + _dot(ob_ref[...], wo_ref[wa:wb, :]) + _dot(om_ref[...], wo_ref[wb:, :])
    y_ref[...] = _layer_norm(alpha * x_ref[...] + att, g_ref[...], b_ref[...])


def _outproj(x2, oa, ob, om, w, alpha, wuv=None):
    n, d = x2.shape
    tm = min(512, n)
    row = lambda wd: pl.BlockSpec((tm, wd), lambda i: (i, 0))
    in_specs = [row(d), row(oa.shape[1]), row(ob.shape[1]), row(om.shape[1])]
    args = [x2, oa, ob, om]
    if wuv is not None:
        in_specs.append(_const_spec(wuv.shape))
        args.append(wuv)
    in_specs += [_const_spec(w["wo"].shape), _const_spec((1, d)), _const_spec((1, d))]
    args += [w["wo"], w["ln1_g"], w["ln1_b"]]
    return pl.pallas_call(
        functools.partial(_outproj_kernel, alpha=alpha, latent=wuv is not None),
        grid=(n // tm,), in_specs=in_specs, out_specs=row(d),
        out_shape=jax.ShapeDtypeStruct((n, d), F32),
        compiler_params=_cparams(1), name="outproj",
    )(*args)


def _ffn_kernel(*refs, alpha, tm, tf, tiles_per_seq, sample, n_tok):
    if sample:
        x_ref, s1_ref, s2_ref, wg_ref, wv_ref, wd_ref, cw_ref, cb_ref, g_ref, b_ref, y_ref, gate_ref, gbuf = refs
    else:
        x_ref, wg_ref, wv_ref, wd_ref, cw_ref, cb_ref, g_ref, b_ref, y_ref, st_ref, gbuf, carry_ref = refs
        first = (pl.program_id(0) % tiles_per_seq) == 0
    x = x_ref[...]
    xb = x.astype(BF16)
    acc = jnp.zeros(x.shape, F32)
    if sample:
        tpos = _iota((tm, tf), 0) % n_tok
        gbuf[0:SUBLANES, :] = jnp.zeros((SUBLANES, D_FF), F32)
    else:
        @pl.when(first)
        def _():
            gbuf[0:SUBLANES, :] = jnp.zeros((SUBLANES, D_FF), F32)

        @pl.when(jnp.logical_not(first))
        def _():
            gbuf[0:SUBLANES, :] = carry_ref[...]
    for c in range(D_FF // tf):
        cs = slice(c * tf, (c + 1) * tf)
        gate = _dot(xb, wg_ref[:, cs])
        val = _dot(xb, wv_ref[:, cs])
        gbuf[SUBLANES:SUBLANES + tm, cs] = gate
        g1 = gbuf[SUBLANES - 1:SUBLANES - 1 + tm, cs]
        g2 = gbuf[SUBLANES - 2:SUBLANES - 2 + tm, cs]
        if sample:
            gate_ref[:, cs] = gate
            g1 = jnp.where(tpos >= 1, g1, s1_ref[:, cs])
            g2 = jnp.where(tpos >= 2, g2, s2_ref[:, cs])
        else:
            carry_ref[:, cs] = gate[tm - SUBLANES:, :]
            st_ref[0, :, cs] = gate[tm - (CONV_W - 1):, :]
        pre = cb_ref[:, cs] + cw_ref[0:1, cs] * g2 + cw_ref[1:2, cs] * g1 + cw_ref[2:3, cs] * gate
        hid = (_gelu_tanh(pre) * val).astype(BF16)
        acc = acc + _dot(hid, wd_ref[cs, :])
    y_ref[...] = _layer_norm(alpha * x + acc, g_ref[...], b_ref[...])


def _ffn(x2, w, alpha, seq_len, prev=None):
    n, d = x2.shape
    tm = min(256, n) if prev is not None else min(512, seq_len)
    tf = 256
    row = lambda wd: pl.BlockSpec((tm, wd), lambda i: (i, 0))
    wspecs = [_const_spec(w["wg"].shape), _const_spec(w["wv_up"].shape), _const_spec(w["wd"].shape),
              _const_spec((CONV_W, D_FF)), _const_spec((1, D_FF)), _const_spec((1, d)), _const_spec((1, d))]
    wargs = [w["wg"], w["wv_up"], w["wd"], w["cw"], w["cb"], w["ln2_g"], w["ln2_b"]]
    if prev is None:
        tiles_per_seq = seq_len // tm
        n_seq = n // seq_len
        kern = functools.partial(_ffn_kernel, alpha=alpha, tm=tm, tf=tf, tiles_per_seq=tiles_per_seq, sample=False, n_tok=0)
        return pl.pallas_call(
            kern, grid=(n // tm,), in_specs=[row(d)] + wspecs,
            out_specs=[row(d), pl.BlockSpec((1, CONV_W - 1, D_FF), lambda i: (i // tiles_per_seq, 0, 0))],
            out_shape=[jax.ShapeDtypeStruct((n, d), F32), jax.ShapeDtypeStruct((n_seq, CONV_W - 1, D_FF), F32)],
            scratch_shapes=[pltpu.VMEM((tm + SUBLANES, D_FF), F32), pltpu.VMEM((SUBLANES, D_FF), F32)],
            compiler_params=_cparams(1), name="ffn_prompt",
        )(x2, *wargs)
    s1, s2 = prev
    kern = functools.partial(_ffn_kernel, alpha=alpha, tm=tm, tf=tf, tiles_per_seq=1, sample=True, n_tok=seq_len)
    return pl.pallas_call(
        kern, grid=(n // tm,), in_specs=[row(d), row(D_FF), row(D_FF)] + wspecs,
        out_specs=[row(d), row(D_FF)],
        out_shape=[jax.ShapeDtypeStruct((n, d), F32), jax.ShapeDtypeStruct((n, D_FF), F32)],
        scratch_shapes=[pltpu.VMEM((tm + SUBLANES, D_FF), F32)],
        compiler_params=_cparams(1), name="ffn_sample",
    )(x2, s1, s2, *wargs)


def _pack_layer(l, w_in, mla_q_norm, mla_kv_norm, w_uq, w_uk, w_uv, w_mem_kv, w_out, ln1_g, ln1_b, w_up, conv_w,
                conv_b, w_down, ln2_g, ln2_b):
    wi = w_in[l]
    d = wi.shape[0]
    o = [0, Q_LORA, Q_LORA + KV_LORA, Q_LORA + KV_LORA + QK_ROPE]
    o += [o[3] + H_B * D_B, o[3] + 2 * H_B * D_B, o[3] + 3 * H_B * D_B]
    cq, ckv, kpe = wi[:, :o[1]], wi[:, o[1]:o[2]], wi[:, o[2]:o[3]]
    qb, kb, vb, qm = wi[:, o[3]:o[4]], wi[:, o[4]:o[5]], wi[:, o[5]:o[6]], wi[:, o[6]:]
    k1, k2 = kpe[:, :ROPE_HALF], kpe[:, ROPE_HALF:]
    zpad = jnp.zeros((d, LANES - QK_ROPE), wi.dtype)
    win = jnp.concatenate([cq, ckv, qb, kb, vb, qm, k1, k2, zpad, k2, k1, zpad], axis=1).astype(BF16)
    uq = w_uq[l].reshape(Q_LORA, H_A, QK_NOPE + QK_ROPE)
    nope, p1, p2 = uq[..., :QK_NOPE], uq[..., QK_NOPE:QK_NOPE + ROPE_HALF], uq[..., QK_NOPE + ROPE_HALF:]
    z32 = jnp.zeros((Q_LORA, H_A, 64 - QK_ROPE), uq.dtype)
    z96 = jnp.zeros((Q_LORA, H_A, LANES - QK_ROPE), uq.dtype)
    wa = jnp.concatenate([p1, p2, z32, nope], axis=2).reshape(Q_LORA, H_A * LANES).astype(BF16)
    wb = jnp.concatenate([p2, p1, z96], axis=2).reshape(Q_LORA, H_A * LANES).astype(BF16)
    uk, uv = w_uk[l], w_uv[l]
    wk = jnp.concatenate([jnp.zeros_like(uk), uk], axis=2).reshape(KV_LORA, H_A * LANES).astype(BF16)
    wv = jnp.concatenate([uv, jnp.zeros_like(uv)], axis=2).reshape(KV_LORA, H_A * LANES).astype(BF16)
    uk_t = jnp.transpose(uk, (1, 2, 0))
    wql = jnp.concatenate([jnp.zeros_like(uk_t), uk_t], axis=1).astype(BF16)
    eye = jnp.eye(H_A, dtype=uv.dtype)
    wuv_bd = (eye[:, None, :, None] * jnp.transpose(uv, (1, 0, 2))[:, :, None, :]).reshape(H_A * KV_LORA, H_A * V_A).astype(BF16)
    up = w_up[l]
    return dict(
        win=win, qn=mla_q_norm[l][None], kvn=mla_kv_norm[l][None], wa=wa, wb=wb, wk=wk, wv=wv, wql=wql, wuv_bd=wuv_bd,
        wmem=w_mem_kv[l].astype(BF16), wo=w_out[l].astype(BF16), ln1_g=ln1_g[l][None], ln1_b=ln1_b[l][None],
        wg=up[:, :D_FF].astype(BF16), wv_up=up[:, D_FF:].astype(BF16), wd=w_down[l].astype(BF16),
        cw=conv_w[l], cb=conv_b[l][None], ln2_g=ln2_g[l][None], ln2_b=ln2_b[l][None])


def _rope_tables(pos):
    inv = ROPE_THETA ** (-jnp.arange(0, QK_ROPE, 2, dtype=F32) / QK_ROPE)
    ang = pos.astype(F32)[:, None] * inv
    cos, sin = jnp.cos(ang), jnp.sin(ang)
    n = pos.shape[0]
    ones = jnp.ones((n, LANES - QK_ROPE), F32)
    return (jnp.concatenate([cos, cos, ones], axis=1), jnp.concatenate([-sin, sin, 0.0 * ones], axis=1))


def kernel(x_prompt, x_sample, cache_mla_ckv, cache_mla_kpe, cache_moba_k, cache_moba_v, cache_mem_k, cache_mem_v, state_conv, page_table, mem_prompt, w_in, mla_q_norm, mla_kv_norm, w_uq, w_uk, w_uv, w_mem_kv, w_out, ln1_g, ln1_b, w_up, conv_w, conv_b, w_down, ln2_g, ln2_b):
    b, s, d = x_prompt.shape
    ns, t_s, _ = x_sample.shape
    depth = w_in.shape[0]
    mem_len = mem_prompt.shape[1]
    n_pages, page = page_table.shape[1], cache_mla_ckv.shape[2]
    past = n_pages * page
    assert s % MOBA_BLOCK == 0 and s // MOBA_BLOCK <= MAX_MOBA_BLOCKS and past % MOBA_BLOCK == 0
    assert t_s == SUBLANES and past // MOBA_BLOCK <= LANES
    alpha = (2 * depth) ** 0.25

    cache_kpt = jnp.transpose(cache_mla_kpe, (0, 1, 3, 2))
    cache_kt = jnp.transpose(cache_moba_k, (0, 1, 3, 4, 2))
    cache_vt = jnp.transpose(cache_moba_v, (0, 1, 3, 4, 2))
    mem_kt = jnp.transpose(cache_mem_k, (0, 1, 3, 4, 2))
    mem_vt = jnp.transpose(cache_mem_v, (0, 1, 3, 4, 2))

    cos_p, sin_p = _rope_tables(jnp.arange(s))
    tm_s = min(512, ns * t_s)
    cos_s, sin_s = _rope_tables(jnp.tile(past + jnp.arange(t_s), tm_s // t_s))
    blk_of = jnp.arange(s) // MOBA_BLOCK
    avg = jnp.where((jnp.arange(LANES)[:, None] - ONEHOT0) == blk_of[None, :], 1.0 / MOBA_BLOCK, 0.0).astype(BF16)

    hp = x_prompt.reshape(b * s, d)
    hs = x_sample.reshape(ns * t_s, d)
    mem2 = mem_prompt.reshape(b * mem_len, d)
    outs = {k: [] for k in ("p_ckv", "p_kpe", "p_k", "p_v", "p_mk", "p_mv", "p_cv", "s_ckv", "s_kpe", "s_k", "s_v", "s_cv")}
    for l in range(depth):
        w = _pack_layer(l, w_in, mla_q_norm, mla_kv_norm, w_uq, w_uk, w_uv, w_mem_kv, w_out, ln1_g, ln1_b, w_up,
                        conv_w, conv_b, w_down, ln2_g, ln2_b)
        ckv, kpe, kb, vb, qa, ka, va, qbp, kbp, vbb, qmp = _proj_prompt(hp, w, cos_p, sin_p, s)
        mk, mv, mkp, mvb = _memkv(mem2, w["wmem"])
        r3 = lambda a: a.reshape(b, -1, a.shape[-1])
        o_a = _flash(r3(qa), r3(ka), r3(va), "causal")
        o_b = _flash(r3(qbp), r3(kbp), r3(vbb), "moba", avg)
        o_m = _flash(r3(qmp), r3(mkp), r3(mvb), "full")
        f2 = lambda a: a.reshape(b * s, a.shape[-1])
        hp = _outproj(hp, f2(o_a), f2(o_b), f2(o_m), w, alpha)
        hp, cv = _ffn(hp, w, alpha, s)
        outs["p_ckv"].append(ckv.reshape(b, s, KV_LORA))
        outs["p_kpe"].append(kpe.reshape(b, s, QK_ROPE))
        outs["p_k"].append(kb.reshape(b, s, H_B, D_B))
        outs["p_v"].append(vb.reshape(b, s, H_B, D_B))
        outs["p_mk"].append(mk.reshape(b, mem_len, H_M, D_M))
        outs["p_mv"].append(mv.reshape(b, mem_len, H_M, D_M))
        outs["p_cv"].append(cv)
        ckv, kpe, kb, vb, qa, qlat, qb, qm = _proj_sample(hs, w, cos_s, sin_s)
        o_lat = _mla_sample(page_table, qlat.reshape(ns, t_s * H_A, KV_LORA), qa.reshape(ns, t_s * H_A, LANES),
                            ckv.reshape(ns, t_s, KV_LORA), kpe.reshape(ns, t_s, QK_ROPE), cache_mla_ckv, cache_kpt, l)
        t3 = lambda a: a.reshape(ns, t_s, a.shape[-1])
        o_b, o_m = _moba_sample(page_table, t3(qb), t3(kb), t3(vb), t3(qm), mem_kt, mem_vt, cache_kt, cache_vt, l)
        hs = _outproj(hs, o_lat.reshape(ns * t_s, H_A * KV_LORA), o_b.reshape(ns * t_s, -1), o_m.reshape(ns * t_s, -1),
                      w, alpha, wuv=w["wuv_bd"])
        prev = state_conv[l]
        zrow = jnp.zeros((ns, t_s - 1, D_FF), F32)
        s1 = jnp.concatenate([prev[:, 1:2], zrow], axis=1).reshape(ns * t_s, D_FF)
        s2 = jnp.concatenate([prev, zrow[:, 1:]], axis=1).reshape(ns * t_s, D_FF)
        hs, gate = _ffn(hs, w, alpha, t_s, prev=(s1, s2))
        outs["s_ckv"].append(ckv.reshape(ns, t_s, KV_LORA))
        outs["s_kpe"].append(kpe.reshape(ns, t_s, QK_ROPE))
        outs["s_k"].append(kb.reshape(ns, t_s, H_B, D_B))
        outs["s_v"].append(vb.reshape(ns, t_s, H_B, D_B))
        outs["s_cv"].append(gate.reshape(ns, t_s, D_FF)[:, t_s - (CONV_W - 1):])
    st = lambda k: jnp.stack(outs[k])
    return (hp.reshape(b, s, d), hs.reshape(ns, t_s, d),
            st("p_ckv"), st("p_kpe"), st("p_k"), st("p_v"), st("p_mk"), st("p_mv"), st("p_cv"),
            st("s_ckv"), st("s_kpe"), st("s_k"), st("s_v"), st("s_cv"))
```

```python
import functools

import jax
import jax.numpy as jnp
from jax import lax
from jax.experimental import pallas as pl
from jax.experimental.pallas import tpu as pltpu

F32 = jnp.float32
BF16 = jnp.bfloat16

H_A, QK_NOPE, QK_ROPE, V_A = 8, 64, 32, 64
Q_LORA, KV_LORA = 384, 256
H_B, D_B, MOBA_BLOCK, MOBA_TOPK = 4, 64, 256, 3
H_M, D_M = 4, 64
D_FF, CONV_W = 2816, 3
ROPE_THETA = 10000.0
NORM_EPS = 1e-5
RMS_EPS = 1e-6
NEG = -1e30

LANES = 128
SUBLANES = 8
VMEM_LIMIT = 56 * 1024 * 1024

ROPE_HALF = QK_ROPE // 2
AUG0 = 64
ONEHOT0 = 80
MAX_MOBA_BLOCKS = 16

C_CQ, C_CKV, C_QB, C_KB, C_VB, C_QM, C_KPA, C_KPB, C_END = 0, 384, 640, 896, 1152, 1408, 1664, 1792, 1920


def _cparams(n_axes):
    return pltpu.CompilerParams(dimension_semantics=("arbitrary",) * n_axes, vmem_limit_bytes=VMEM_LIMIT)


def _const_spec(shape):
    nd = len(shape)
    return pl.BlockSpec(shape, lambda *_: (0,) * nd, pipeline_mode=pl.Buffered(1))


def _dot(a, b):
    return jnp.dot(a, b, preferred_element_type=F32)


def _dot_nt(a, b):
    return lax.dot_general(a, b, (((1,), (1,)), ((), ())), preferred_element_type=F32)


def _iota(shape, dim):
    return lax.broadcasted_iota(jnp.int32, shape, dim)


def _rms(x, g):
    return x * lax.rsqrt(jnp.mean(x * x, axis=-1, keepdims=True) + RMS_EPS) * g


def _layer_norm(x, g, b):
    mu = jnp.mean(x, axis=-1, keepdims=True)
    xc = x - mu
    var = jnp.mean(xc * xc, axis=-1, keepdims=True)
    return xc * lax.rsqrt(var + NORM_EPS) * g + b


def _gelu_tanh(x):
    return x * (0.5 * (1.0 + jnp.tanh(0.7978845608028654 * (x + 0.044715 * (x * x * x)))))


def _head_to_lanes(seg, h):
    blk = seg[:, (h // 2) * LANES:(h // 2 + 1) * LANES]
    if h % 2:
        blk = pltpu.roll(blk, 64, axis=1)
    return jnp.where(_iota(blk.shape, 1) < 64, blk, 0.0)


def _ones_hi(m):
    return jnp.where(_iota((m, LANES), 1) >= 64, 1.0, 0.0)


def _slope(h):
    return 2.0 ** (-8.0 * (h + 1) / H_B)


def _proj_common(x_ref, win_ref, qn_ref, kvn_ref, wa_ref, wb_ref, c_ref, s_ref, ckv_ref, kpe_ref, kb_ref, vb_ref, qa_ref):
    xb = x_ref[...].astype(BF16)

    def seg(a, b):
        return _dot(xb, win_ref[:, a:b])

    cos_t, sin_t = c_ref[...], s_ref[...]
    scale_a = (QK_NOPE + QK_ROPE) ** -0.5
    cqn = _rms(seg(C_CQ, C_CKV), qn_ref[...]).astype(BF16)
    for h in range(H_A):
        hs = slice(h * LANES, (h + 1) * LANES)
        q_rot = _dot(cqn, wa_ref[:, hs]) * cos_t + _dot(cqn, wb_ref[:, hs]) * sin_t
        qa_ref[:, hs] = (q_rot * scale_a).astype(BF16)
    ckvn = _rms(seg(C_CKV, C_QB), kvn_ref[...])
    ckv_ref[...] = ckvn
    kpe_rot = seg(C_KPA, C_KPB) * cos_t + seg(C_KPB, C_END) * sin_t
    kpe_ref[...] = kpe_rot[:, :QK_ROPE]
    qb, kb, vb, qm = seg(C_QB, C_KB), seg(C_KB, C_VB), seg(C_VB, C_QM), seg(C_QM, C_KPA)
    kb_ref[...] = kb
    vb_ref[...] = vb
    return ckvn.astype(BF16), kpe_rot, qb, kb, vb, qm


def _proj_prompt_kernel(x_ref, win_ref, qn_ref, kvn_ref, wa_ref, wb_ref, wk_ref, wv_ref, c_ref, s_ref,
                        ckv_ref, kpe_ref, kb_ref, vb_ref, qa_ref, ka_ref, va_ref, qbp_ref, kbp_ref, vbb_ref, qmp_ref,
                        *, tm, tiles_per_seq):
    ckvb, kpe_rot, qb, kb, vb, qm = _proj_common(x_ref, win_ref, qn_ref, kvn_ref, wa_ref, wb_ref, c_ref, s_ref,
                                                 ckv_ref, kpe_ref, kb_ref, vb_ref, qa_ref)
    ones_hi = _ones_hi(tm)
    for h in range(H_A):
        hs = slice(h * LANES, (h + 1) * LANES)
        ka_ref[:, hs] = (_dot(ckvb, wk_ref[:, hs]) + kpe_rot).astype(BF16)
        va_ref[:, hs] = (_dot(ckvb, wv_ref[:, hs]) + ones_hi).astype(BF16)
    for h in range(H_B):
        vbb_ref[:, h * LANES:(h + 1) * LANES] = (_head_to_lanes(vb, h) + ones_hi).astype(BF16)
    pos = (pl.program_id(0) % tiles_per_seq) * tm + _iota((tm, LANES), 0)
    lane = _iota((tm, LANES), 1)
    blk = (pos >> 8).astype(F32)
    rem = (pos & (MOBA_BLOCK - 1)).astype(F32)
    k_aug = jnp.where(lane < AUG0 + 2, 1.0, jnp.where(lane == AUG0 + 2, blk, rem))
    k_aug = jnp.where((lane >= AUG0) & (lane < AUG0 + 4), k_aug, 0.0)
    k_aug = jnp.where((lane - ONEHOT0) == (pos >> 8), 1.0, k_aug)
    q_aug = jnp.where(lane == AUG0, -float(MOBA_BLOCK) * blk,
                      jnp.where(lane == AUG0 + 1, -rem, jnp.where(lane == AUG0 + 2, float(MOBA_BLOCK), 1.0)))
    q_aug = jnp.where((lane >= AUG0) & (lane < AUG0 + 4), q_aug, 0.0)
    for h in range(H_B):
        hs = slice(h * LANES, (h + 1) * LANES)
        qbp_ref[:, hs] = (_head_to_lanes(qb, h) * (D_B ** -0.5) + _slope(h) * q_aug).astype(BF16)
        kbp_ref[:, hs] = (_head_to_lanes(kb, h) + k_aug).astype(BF16)
        qmp_ref[:, hs] = (_head_to_lanes(qm, h) * (D_M ** -0.5)).astype(BF16)


def _proj_sample_kernel(x_ref, win_ref, qn_ref, kvn_ref, wa_ref, wb_ref, wql_ref, c_ref, s_ref,
                        ckv_ref, kpe_ref, kb_ref, vb_ref, qa_ref, qlat_ref, qb_ref, qm_ref):
    _, _, qb, _, _, qm = _proj_common(x_ref, win_ref, qn_ref, kvn_ref, wa_ref, wb_ref, c_ref, s_ref,
                                      ckv_ref, kpe_ref, kb_ref, vb_ref, qa_ref)
    for h in range(H_A):
        qlat_ref[:, h * KV_LORA:(h + 1) * KV_LORA] = _dot(qa_ref[:, h * LANES:(h + 1) * LANES], wql_ref[h]).astype(BF16)
    qb_ref[...] = qb * (D_B ** -0.5)
    qm_ref[...] = qm * (D_M ** -0.5)


def _proj_specs(tm, n_tab):
    row = lambda w: pl.BlockSpec((tm, w), lambda i: (i, 0))
    tab = pl.BlockSpec((tm, LANES), lambda i: (i % n_tab, 0))
    return row, tab


def _proj_prompt(x2, w, cos_t, sin_t, seq_len):
    n, d = x2.shape
    tm = min(512, seq_len)
    tiles_per_seq = seq_len // tm
    row, tab = _proj_specs(tm, tiles_per_seq)
    out_w = [(KV_LORA, F32), (QK_ROPE, F32), (H_B * D_B, F32), (H_B * D_B, F32), (H_A * LANES, BF16), (H_A * LANES, BF16),
             (H_A * LANES, BF16), (H_B * LANES, BF16), (H_B * LANES, BF16), (H_B * LANES, BF16), (H_M * LANES, BF16)]
    return pl.pallas_call(
        functools.partial(_proj_prompt_kernel, tm=tm, tiles_per_seq=tiles_per_seq),
        grid=(n // tm,),
        in_specs=[row(d), _const_spec(w["win"].shape), _const_spec((1, Q_LORA)), _const_spec((1, KV_LORA)),
                  _const_spec(w["wa"].shape), _const_spec(w["wb"].shape), _const_spec(w["wk"].shape),
                  _const_spec(w["wv"].shape), tab, tab],
        out_specs=[row(wd) for wd, _ in out_w],
        out_shape=[jax.ShapeDtypeStruct((n, wd), dt) for wd, dt in out_w],
        compiler_params=_cparams(1), name="proj_prompt",
    )(x2, w["win"], w["qn"], w["kvn"], w["wa"], w["wb"], w["wk"], w["wv"], cos_t, sin_t)


def _proj_sample(x2, w, cos_t, sin_t):
    n, d = x2.shape
    tm = min(512, n)
    row, tab = _proj_specs(tm, cos_t.shape[0] // tm)
    out_w = [(KV_LORA, F32), (QK_ROPE, F32), (H_B * D_B, F32), (H_B * D_B, F32), (H_A * LANES, BF16),
             (H_A * KV_LORA, BF16), (H_B * D_B, F32), (H_M * D_M, F32)]
    return pl.pallas_call(
        _proj_sample_kernel,
        grid=(n // tm,),
        in_specs=[row(d), _const_spec(w["win"].shape), _const_spec((1, Q_LORA)), _const_spec((1, KV_LORA)),
                  _const_spec(w["wa"].shape), _const_spec(w["wb"].shape), _const_spec(w["wql"].shape), tab, tab],
        out_specs=[row(wd) for wd, _ in out_w],
        out_shape=[jax.ShapeDtypeStruct((n, wd), dt) for wd, dt in out_w],
        compiler_params=_cparams(1), name="proj_sample",
    )(x2, w["win"], w["qn"], w["kvn"], w["wa"], w["wb"], w["wql"], cos_t, sin_t)


def _memkv_kernel(x_ref, w_ref, mk_ref, mv_ref, mkp_ref, mvb_ref):
    kv = _dot(x_ref[...].astype(BF16), w_ref[...])
    mk, mv = kv[:, :H_M * D_M], kv[:, H_M * D_M:]
    mk_ref[...] = mk
    mv_ref[...] = mv
    ones_hi = _ones_hi(mk.shape[0])
    for h in range(H_M):
        mkp_ref[:, h * LANES:(h + 1) * LANES] = _head_to_lanes(mk, h).astype(BF16)
        mvb_ref[:, h * LANES:(h + 1) * LANES] = (_head_to_lanes(mv, h) + ones_hi).astype(BF16)


def _memkv(mem2, w_bf):
    n, d = mem2.shape
    tm = min(256, n)
    row = lambda w: pl.BlockSpec((tm, w), lambda i: (i, 0))
    out_w = [(H_M * D_M, F32), (H_M * D_M, F32), (H_M * LANES, BF16), (H_M * LANES, BF16)]
    return pl.pallas_call(
        _memkv_kernel, grid=(n // tm,),
        in_specs=[row(d), _const_spec(w_bf.shape)],
        out_specs=[row(wd) for wd, _ in out_w],
        out_shape=[jax.ShapeDtypeStruct((n, wd), dt) for wd, dt in out_w],
        compiler_params=_cparams(1), name="mem_kv",
    )(mem2, w_bf)


def _moba_bias(q, km, qi):
    tq = q.shape[0]
    gate_t = _dot_nt(km, q)[ONEHOT0:ONEHOT0 + MAX_MOBA_BLOCKS, :]
    c = _iota(gate_t.shape, 0)
    valid = c < qi
    gm = jnp.where(valid, gate_t, -jnp.inf)
    ahead = jnp.zeros(gate_t.shape, F32)
    for cp in range(MAX_MOBA_BLOCKS):
        r = gm[cp:cp + 1, :]
        ahead = ahead + jnp.where(r > gm, 1.0, jnp.where(r == gm, jnp.where(cp < c, 1.0, 0.0), 0.0))
    keep = jnp.where(valid, jnp.where(ahead < MOBA_TOPK, 0.0, NEG), jnp.where(c == qi, 0.0, NEG))
    keep_t = jnp.concatenate([jnp.zeros((ONEHOT0, tq), F32), keep,
                              jnp.zeros((LANES - ONEHOT0 - MAX_MOBA_BLOCKS, tq), F32)], axis=0)
    bias = keep_t.T
    lane = _iota(bias.shape, 1)
    return jnp.where((lane >= ONEHOT0) & (lane < ONEHOT0 + MAX_MOBA_BLOCKS), bias.astype(q.dtype), q)


def _flash_kernel(*refs, tq, tk, mode, n_kblocks, nh):
    if mode == "moba":
        q_ref, k_ref, v_ref, avg_ref, o_ref, s_ref, mx_ref, acc_ref, km_ref = refs
    else:
        q_ref, k_ref, v_ref, o_ref, s_ref, mx_ref, acc_ref = refs
    qi = pl.program_id(2)
    pieces = tk // LANES
    hsl = [slice(hh * LANES, (hh + 1) * LANES) for hh in range(nh)]
    qs = []
    for hh in range(nh):
        q = q_ref[0, :, hsl[hh]]
        if mode == "moba":
            @pl.when(qi == 0)
            def _():
                km = _dot(avg_ref[...], k_ref[0, :, hsl[hh]])
                km_ref[hh] = jnp.where(_iota(km.shape, 1) < D_B, km, 0.0).astype(BF16)

            q = _moba_bias(q, km_ref[hh], qi)
        qs.append(q)

    def score(slot, j, first):
        src = pl.multiple_of(j * tk, tk)
        dst = pl.multiple_of(slot * tk, tk)
        for hh in range(nh):
            s = _dot_nt(qs[hh], k_ref[0, pl.ds(src, tk), hsl[hh]])
            if first and mode != "full":
                s = jnp.where(qi * tq + _iota(s.shape, 0) >= src + _iota(s.shape, 1), s, NEG)
            s_ref[hh, :, pl.ds(dst, tk)] = s
            f = _fold_lanes(s, jnp.maximum)
            mx_ref[hh] = f if first else jnp.maximum(mx_ref[hh], f)

    def attend(slot, j, first):
        src = pl.multiple_of(j * tk, tk)
        dst = pl.multiple_of(slot * tk, tk)
        for hh in range(nh):
            mb = mx_ref[hh]
            ps = [jnp.exp(s_ref[hh, :, pl.ds(dst + i * LANES, LANES)] - mb).astype(BF16) for i in range(pieces)]
            p = jnp.concatenate(ps, axis=1) if pieces > 1 else ps[0]
            pv = _dot(p, v_ref[0, pl.ds(src, tk), hsl[hh]])
            acc_ref[hh] = pv if first else acc_ref[hh] + pv

    def sweep(fn):
        if mode == "full":
            fn(0, 0, True)
            lax.fori_loop(1, n_kblocks, lambda t, c: (fn(t, t, False), c)[1], 0)
        else:
            last = (qi * tq) // tk
            fn(0, last, True)
            lax.fori_loop(1, last + 1, lambda t, c: (fn(t, t - 1, False), c)[1], 0)

    sweep(score)
    for hh in range(nh):
        mx_ref[hh] = jnp.broadcast_to(jnp.max(mx_ref[hh], axis=1, keepdims=True), (tq, LANES))
    sweep(attend)
    low = _iota((tq, LANES), 1) < 64
    for pp in range(nh // 2):
        a0, a1 = acc_ref[2 * pp], acc_ref[2 * pp + 1]
        o0 = a0 / pltpu.roll(a0, 64, axis=1)
        o1 = pltpu.roll(a1, 64, axis=1) / a1
        o_ref[0, :, pp * LANES:(pp + 1) * LANES] = jnp.where(low, o0, o1).astype(o_ref.dtype)


def _flash(q, k, v, mode, avg=None):
    b, s, w = q.shape
    sk = k.shape[1]
    n_heads = w // LANES
    nh = min(4, n_heads)
    tq = min(4 * MOBA_BLOCK, s) if mode == "full" else min(MOBA_BLOCK, s)
    tk = min(MOBA_BLOCK, sk) if mode == "full" else min(2 * MOBA_BLOCK, sk)
    assert sk % tk == 0 and n_heads % nh == 0 and nh % 2 == 0
    in_specs = [pl.BlockSpec((1, tq, nh * LANES), lambda bi, g, i: (bi, i, g)),
                pl.BlockSpec((1, sk, nh * LANES), lambda bi, g, i: (bi, 0, g)),
                pl.BlockSpec((1, sk, nh * LANES), lambda bi, g, i: (bi, 0, g))]
    args = [q, k, v]
    scratch = [pltpu.VMEM((nh, tq, sk), F32)] + [pltpu.VMEM((nh, tq, LANES), F32)] * 2
    if mode == "moba":
        in_specs.append(_const_spec(avg.shape))
        args.append(avg)
        scratch.append(pltpu.VMEM((nh, LANES, LANES), BF16))
    return pl.pallas_call(
        functools.partial(_flash_kernel, tq=tq, tk=tk, mode=mode, n_kblocks=sk // tk, nh=nh),
        grid=(b, n_heads // nh, s // tq),
        in_specs=in_specs,
        out_specs=pl.BlockSpec((1, tq, nh * 64), lambda bi, g, i: (bi, i, g)),
        out_shape=jax.ShapeDtypeStruct((b, s, n_heads * 64), BF16),
        scratch_shapes=scratch,
        compiler_params=_cparams(3), name="flash_" + mode,
    )(*args)


def _fold_lanes(x, op):
    out = x[:, :LANES]
    for i in range(1, x.shape[1] // LANES):
        out = op(out, x[:, i * LANES:(i + 1) * LANES])
    return out


def _mla_sample_kernel(pt_ref, qlat_ref, qpe_ref, cn_ref, kn_ref, ckv_hbm, kpt_hbm, o_ref,
                       ckv_buf, kpt_buf, kc_ref, s_ref, sem, *, layer, n_pages, page, tk, n_tok):
    s = pl.program_id(0)
    ns = pl.num_programs(0)
    slot = s % 2
    rows = n_tok * H_A
    n_chunks = (n_pages * page) // tk
    pages_per_chunk = tk // page

    def copies(pg, sl, p):
        off = pl.multiple_of(p * page, page)
        return (pltpu.make_async_copy(ckv_hbm.at[layer, pg], ckv_buf.at[sl, pl.ds(off, page), :], sem.at[sl, 0]),
                pltpu.make_async_copy(kpt_hbm.at[layer, pg], kpt_buf.at[sl, p], sem.at[sl, 1]))

    def start_pages(seq, sl, p0, n):
        for i in range(n):
            for c in copies(pt_ref[seq, p0 + i], sl, p0 + i):
                c.start(priority=i % 2)

    @pl.when(s == 0)
    def _():
        lax.fori_loop(0, n_chunks, lambda c, x: (start_pages(0, 0, c * pages_per_chunk, pages_per_chunk), x)[1], 0)

    for p in range(n_pages):
        for c in copies(0, slot, p):
            c.wait()

    ql = qlat_ref[0]
    qp = qpe_ref[0][:, :QK_ROPE]

    def score_chunk(c, mx):
        @pl.when(s + 1 < ns)
        def _():
            start_pages(s + 1, 1 - slot, c * pages_per_chunk, pages_per_chunk)

        off = pl.multiple_of(c * tk, tk)
        kc = ckv_buf[slot, pl.ds(off, tk), :].astype(BF16)
        kc_ref[pl.ds(off, tk), :] = kc
        kp = jnp.concatenate([kpt_buf[slot, c * pages_per_chunk + i].astype(BF16) for i in range(pages_per_chunk)], axis=1)
        sc = _dot_nt(ql, kc) + _dot(qp, kp)
        s_ref[:, pl.ds(off, tk)] = sc
        return jnp.maximum(mx, _fold_lanes(sc, jnp.maximum))

    pad = jnp.zeros((LANES - n_tok, KV_LORA), F32)
    cn = jnp.concatenate([cn_ref[0], pad], axis=0).astype(BF16)
    kn = jnp.concatenate([kn_ref[0], pad[:, :QK_ROPE]], axis=0).astype(BF16)
    s0 = _dot_nt(ql, cn) + _dot_nt(qp, kn)
    s0 = jnp.where(_iota(s0.shape, 1) <= _iota(s0.shape, 0) // H_A, s0, NEG)
    mx = lax.fori_loop(0, n_chunks, score_chunk, s0, unroll=min(8, n_chunks))
    mb = jnp.broadcast_to(jnp.max(mx, axis=1, keepdims=True), (rows, LANES))

    def attend_chunk(c, carry):
        ls, acc = carry
        off = pl.multiple_of(c * tk, tk)
        ps = [jnp.exp(s_ref[:, pl.ds(off + i * LANES, LANES)] - mb) for i in range(tk // LANES)]
        for x in ps:
            ls = ls + x
        p = jnp.concatenate([x.astype(BF16) for x in ps], axis=1)
        return ls, acc + _dot(p, kc_ref[pl.ds(off, tk), :])

    p0 = jnp.exp(s0 - mb)
    ls, acc = lax.fori_loop(0, n_chunks, attend_chunk, (p0, _dot(p0.astype(BF16), cn)), unroll=min(8, n_chunks))
    o_ref[0] = (acc / jnp.sum(ls, axis=1, keepdims=True)).astype(o_ref.dtype)


def _mla_sample(page_table, qlat3, qpe3, ckv_new3, kpe_new3, cache_ckv, cache_kpt, layer):
    ns, n_pages = page_table.shape
    page = cache_ckv.shape[2]
    n_tok = ckv_new3.shape[1]
    rows = n_tok * H_A
    t_past = n_pages * page
    tk = min(512, t_past)
    blk = lambda r, w: pl.BlockSpec((1, r, w), lambda s, pt: (s, 0, 0))
    return pl.pallas_call(
        functools.partial(_mla_sample_kernel, layer=layer, n_pages=n_pages, page=page, tk=tk, n_tok=n_tok),
        grid_spec=pltpu.PrefetchScalarGridSpec(
            num_scalar_prefetch=1, grid=(ns,),
            in_specs=[blk(rows, KV_LORA), blk(rows, LANES), blk(n_tok, KV_LORA), blk(n_tok, QK_ROPE),
                      pl.BlockSpec(memory_space=pl.ANY), pl.BlockSpec(memory_space=pl.ANY)],
            out_specs=blk(rows, KV_LORA),
            scratch_shapes=[pltpu.VMEM((2, t_past, KV_LORA), F32), pltpu.VMEM((2, n_pages, QK_ROPE, page), F32),
                            pltpu.VMEM((t_past, KV_LORA), BF16), pltpu.VMEM((rows, t_past), F32),
                            pltpu.SemaphoreType.DMA((2, 2))]),
        out_shape=jax.ShapeDtypeStruct((ns, rows, KV_LORA), BF16),
        compiler_params=_cparams(1), name="mla_sample",
    )(page_table, qlat3, qpe3, ckv_new3, kpe_new3, cache_ckv, cache_kpt)


def _block_diag_rows(x8):
    head = _iota(x8.shape, 1) // D_B
    return jnp.concatenate([jnp.where(head == h, x8, 0.0) for h in range(H_B)], axis=0)


def _diag_heads(acc, n_tok):
    head = _iota((n_tok, acc.shape[1]), 1) // D_B
    out = jnp.zeros((n_tok, acc.shape[1]), F32)
    for h in range(H_B):
        out = out + jnp.where(head == h, acc[h * n_tok:(h + 1) * n_tok, :], 0.0)
    return out


def _moba_sample_kernel(pt_ref, q_ref, kn_ref, vn_ref, qm_ref, mk_ref, mv_ref, kt_hbm, vt_hbm, ob_ref, om_ref,
                        kt_buf, vt_buf, s_buf, sem, *, layer, n_pages, page, n_tok):
    s = pl.program_id(0)
    ns = pl.num_programs(0)
    slot = s % 2
    rows = n_tok * H_B
    t_past = n_pages * page
    n_blocks = t_past // MOBA_BLOCK

    pages_per_block = MOBA_BLOCK // page

    def copies(pg, sl, p):
        return (pltpu.make_async_copy(kt_hbm.at[layer, pg], kt_buf.at[sl, p], sem.at[sl, 0]),
                pltpu.make_async_copy(vt_hbm.at[layer, pg], vt_buf.at[sl, p], sem.at[sl, 1]))

    def block_t(buf, c):
        parts = [buf[slot, c * pages_per_block + i].reshape(H_B * D_B, page).astype(BF16) for i in range(pages_per_block)]
        return jnp.concatenate(parts, axis=1) if len(parts) > 1 else parts[0]

    def start_pages(seq, sl, p0, n):
        for i in range(n):
            for prio, c in enumerate(copies(pt_ref[seq, p0 + i], sl, p0 + i)):
                c.start(priority=prio)

    @pl.when(s == 0)
    def _():
        lax.fori_loop(0, n_blocks, lambda c, x: (start_pages(0, 0, c * pages_per_block, pages_per_block), x)[1], 0)

    qmb = _block_diag_rows(qm_ref[0]).astype(BF16)
    hd = H_M * D_M
    sm = _dot(qmb, mk_ref[0, 0].reshape(hd, -1).astype(BF16))
    pm = jnp.exp(sm - jnp.max(sm, axis=1, keepdims=True))
    om = _dot_nt(pm.astype(BF16), mv_ref[0, 0].reshape(hd, -1).astype(BF16)) / jnp.sum(pm, axis=1, keepdims=True)
    om_ref[0] = _diag_heads(om, n_tok).astype(om_ref.dtype)

    for p in range(n_pages):
        for c in copies(0, slot, p):
            c.wait()

    qbd = _block_diag_rows(q_ref[0]).astype(BF16)
    row = _iota((rows, 1), 0)
    r_head, r_tok = row // n_tok, row % n_tok
    slope = jnp.where(r_head == 0, _slope(0), jnp.where(r_head == 1, _slope(1), jnp.where(r_head == 2, _slope(2), _slope(3))))
    lane = _iota((rows, LANES), 1)

    def score_chunk(c, gates):
        @pl.when(s + 1 < ns)
        def _():
            start_pages(s + 1, 1 - slot, c * pages_per_block, pages_per_block)

        off = pl.multiple_of(c * MOBA_BLOCK, MOBA_BLOCK)
        sr = _dot(qbd, block_t(kt_buf, c))
        s_buf[:, pl.ds(off, MOBA_BLOCK)] = sr
        return jnp.where(lane == c, jnp.sum(_fold_lanes(sr, jnp.add), axis=1, keepdims=True), gates)

    gates = lax.fori_loop(0, n_blocks, score_chunk, jnp.zeros((rows, LANES), F32), unroll=min(8, n_blocks))
    gm = jnp.where(lane < n_blocks, gates, -jnp.inf)
    ahead = jnp.zeros((rows, LANES), F32)
    for cp in range(n_blocks):
        col = gm[:, cp:cp + 1]
        ahead = ahead + jnp.where(col > gm, 1.0, jnp.where(col == gm, jnp.where(cp < lane, 1.0, 0.0), 0.0))
    sel_bias = jnp.where((lane < n_blocks) & (ahead < MOBA_TOPK), 0.0, NEG)

    pad = jnp.zeros((LANES - n_tok, H_B * D_B), F32)
    kn = jnp.concatenate([kn_ref[0], pad], axis=0).astype(BF16)
    vn_t = jnp.concatenate([vn_ref[0], pad], axis=0).T.astype(BF16)
    s0 = _dot_nt(qbd, kn) - slope * (r_tok - lane).astype(F32)
    s0 = jnp.where(lane <= r_tok, s0, NEG)

    alibi0 = slope * (_iota((rows, MOBA_BLOCK), 1) - r_tok - t_past).astype(F32)
    mx = s0
    for c in range(n_blocks):
        cs = slice(c * MOBA_BLOCK, (c + 1) * MOBA_BLOCK)
        sb = s_buf[:, cs] + alibi0 + (sel_bias[:, c:c + 1] + slope * float(c * MOBA_BLOCK))
        s_buf[:, cs] = sb
        mx = jnp.maximum(mx, _fold_lanes(sb, jnp.maximum))
    mb = jnp.broadcast_to(jnp.max(mx, axis=1, keepdims=True), (rows, LANES))

    def attend_chunk(c, carry):
        ls, acc = carry
        off = pl.multiple_of(c * MOBA_BLOCK, MOBA_BLOCK)
        ps = [jnp.exp(s_buf[:, pl.ds(off + i * LANES, LANES)] - mb) for i in range(MOBA_BLOCK // LANES)]
        for x in ps:
            ls = ls + x
        p = jnp.concatenate([x.astype(BF16) for x in ps], axis=1)
        return ls, acc + _dot_nt(p, block_t(vt_buf, c))

    p0 = jnp.exp(s0 - mb)
    ls, acc = lax.fori_loop(0, n_blocks, attend_chunk, (p0, _dot_nt(p0.astype(BF16), vn_t)), unroll=min(8, n_blocks))
    ob_ref[0] = _diag_heads(acc / jnp.sum(ls, axis=1, keepdims=True), n_tok).astype(ob_ref.dtype)


def _moba_sample(page_table, q3, kn3, vn3, qm3, mem_kt, mem_vt, cache_kt, cache_vt, layer):
    ns, n_pages = page_table.shape
    page = cache_kt.shape[-1]
    n_tok = q3.shape[1]
    t_past = n_pages * page
    w = H_B * D_B
    mem_len = mem_kt.shape[-1]
    blk = pl.BlockSpec((1, n_tok, w), lambda s, pt: (s, 0, 0))
    mem_blk = pl.BlockSpec((1, 1, H_M, D_M, mem_len), lambda s, pt: (layer, s, 0, 0, 0))
    return pl.pallas_call(
        functools.partial(_moba_sample_kernel, layer=layer, n_pages=n_pages, page=page, n_tok=n_tok),
        grid_spec=pltpu.PrefetchScalarGridSpec(
            num_scalar_prefetch=1, grid=(ns,),
            in_specs=[blk, blk, blk, blk, mem_blk, mem_blk,
                      pl.BlockSpec(memory_space=pl.ANY), pl.BlockSpec(memory_space=pl.ANY)],
            out_specs=[blk, blk],
            scratch_shapes=[pltpu.VMEM((2, n_pages, H_B, D_B, page), F32), pltpu.VMEM((2, n_pages, H_B, D_B, page), F32),
                            pltpu.VMEM((n_tok * H_B, t_past), F32), pltpu.SemaphoreType.DMA((2, 2))]),
        out_shape=[jax.ShapeDtypeStruct((ns, n_tok, w), BF16)] * 2,
        compiler_params=_cparams(1), name="moba_sample",
    )(page_table, q3, kn3, vn3, qm3, mem_kt, mem_vt, cache_kt, cache_vt)


def _outproj_kernel(*refs, alpha, latent):
    if latent:
        x_ref, oa_ref, ob_ref, om_ref, wuv_ref, wo_ref, g_ref, b_ref, y_ref = refs
        oa = _dot(oa_ref[...], wuv_ref[...]).astype(BF16)
    else:
        x_ref, oa_ref, ob_ref, om_ref, wo_ref, g_ref, b_ref, y_ref = refs
        oa = oa_ref[...]
    wa, wb = H_A * V_A, H_A * V_A + H_B * D_B
    att = _dot(oa, wo_ref[:wa, :]) + _dot(ob_ref[...], wo_ref[wa:wb, :]) + _dot(om_ref[...], wo_ref[wb:, :])
    y_ref[...] = _layer_norm(alpha * x_ref[...] + att, g_ref[...], b_ref[...])


def _outproj(x2, oa, ob, om, w, alpha, wuv=None):
    n, d = x2.shape
    tm = min(512, n)
    row = lambda wd: pl.BlockSpec((tm, wd), lambda i: (i, 0))
    in_specs = [row(d), row(oa.shape[1]), row(ob.shape[1]), row(om.shape[1])]
    args = [x2, oa, ob, om]
    if wuv is not None:
        in_specs.append(_const_spec(wuv.shape))
        args.append(wuv)
    in_specs += [_const_spec(w["wo"].shape), _const_spec((1, d)), _const_spec((1, d))]
    args += [w["wo"], w["ln1_g"], w["ln1_b"]]
    return pl.pallas_call(
        functools.partial(_outproj_kernel, alpha=alpha, latent=wuv is not None),
        grid=(n // tm,), in_specs=in_specs, out_specs=row(d),
        out_shape=jax.ShapeDtypeStruct((n, d), F32),
        compiler_params=_cparams(1), name="outproj",
    )(*args)


def _ffn_kernel(*refs, alpha, tm, tf, tiles_per_seq, sample, n_tok):
    if sample:
        x_ref, s1_ref, s2_ref, wg_ref, wv_ref, wd_ref, cw_ref, cb_ref, g_ref, b_ref, y_ref, gate_ref, gbuf = refs
    else:
        x_ref, wg_ref, wv_ref, wd_ref, cw_ref, cb_ref, g_ref, b_ref, y_ref, st_ref, gbuf, carry_ref = refs
        first = (pl.program_id(0) % tiles_per_seq) == 0
    x = x_ref[...]
    xb = x.astype(BF16)
    acc = jnp.zeros(x.shape, F32)
    if sample:
        tpos = _iota((tm, tf), 0) % n_tok
        gbuf[0:SUBLANES, :] = jnp.zeros((SUBLANES, D_FF), F32)
    else:
        @pl.when(first)
        def _():
            gbuf[0:SUBLANES, :] = jnp.zeros((SUBLANES, D_FF), F32)

        @pl.when(jnp.logical_not(first))
        def _():
            gbuf[0:SUBLANES, :] = carry_ref[...]
    for c in range(D_FF // tf):
        cs = slice(c * tf, (c + 1) * tf)
        gate = _dot(xb, wg_ref[:, cs])
        val = _dot(xb, wv_ref[:, cs])
        gbuf[SUBLANES:SUBLANES + tm, cs] = gate
        g1 = gbuf[SUBLANES - 1:SUBLANES - 1 + tm, cs]
        g2 = gbuf[SUBLANES - 2:SUBLANES - 2 + tm, cs]
        if sample:
            gate_ref[:, cs] = gate
            g1 = jnp.where(tpos >= 1, g1, s1_ref[:, cs])
            g2 = jnp.where(tpos >= 2, g2, s2_ref[:, cs])
        else:
            carry_ref[:, cs] = gate[tm - SUBLANES:, :]
            st_ref[0, :, cs] = gate[tm - (CONV_W - 1):, :]
        pre = cb_ref[:, cs] + cw_ref[0:1, cs] * g2 + cw_ref[1:2, cs] * g1 + cw_ref[2:3, cs] * gate
        hid = (_gelu_tanh(pre) * val).astype(BF16)
        acc = acc + _dot(hid, wd_ref[cs, :])
    y_ref[...] = _layer_norm(alpha * x + acc, g_ref[...], b_ref[...])


def _ffn(x2, w, alpha, seq_len, prev=None):
    n, d = x2.shape
    tm = min(256, n) if prev is not None else min(512, seq_len)
    tf = 2816
    row = lambda wd: pl.BlockSpec((tm, wd), lambda i: (i, 0))
    wspecs = [_const_spec(w["wg"].shape), _const_spec(w["wv_up"].shape), _const_spec(w["wd"].shape),
              _const_spec((CONV_W, D_FF)), _const_spec((1, D_FF)), _const_spec((1, d)), _const_spec((1, d))]
    wargs = [w["wg"], w["wv_up"], w["wd"], w["cw"], w["cb"], w["ln2_g"], w["ln2_b"]]
    if prev is None:
        tiles_per_seq = seq_len // tm
        n_seq = n // seq_len
        kern = functools.partial(_ffn_kernel, alpha=alpha, tm=tm, tf=tf, tiles_per_seq=tiles_per_seq, sample=False, n_tok=0)
        return pl.pallas_call(
            kern, grid=(n // tm,), in_specs=[row(d)] + wspecs,
            out_specs=[row(d), pl.BlockSpec((1, CONV_W - 1, D_FF), lambda i: (i // tiles_per_seq, 0, 0))],
            out_shape=[jax.ShapeDtypeStruct((n, d), F32), jax.ShapeDtypeStruct((n_seq, CONV_W - 1, D_FF), F32)],
            scratch_shapes=[pltpu.VMEM((tm + SUBLANES, D_FF), F32), pltpu.VMEM((SUBLANES, D_FF), F32)],
            compiler_params=_cparams(1), name="ffn_prompt",
        )(x2, *wargs)
    s1, s2 = prev
    kern = functools.partial(_ffn_kernel, alpha=alpha, tm=tm, tf=tf, tiles_per_seq=1, sample=True, n_tok=seq_len)
    return pl.pallas_call(
        kern, grid=(n // tm,), in_specs=[row(d), row(D_FF), row(D_FF)] + wspecs,
        out_specs=[row(d), row(D_FF)],
        out_shape=[jax.ShapeDtypeStruct((n, d), F32), jax.ShapeDtypeStruct((n, D_FF), F32)],
        scratch_shapes=[pltpu.VMEM((tm + SUBLANES, D_FF), F32)],
        compiler_params=_cparams(1), name="ffn_sample",
    )(x2, s1, s2, *wargs)


def _pack_layer(l, w_in, mla_q_norm, mla_kv_norm, w_uq, w_uk, w_uv, w_mem_kv, w_out, ln1_g, ln1_b, w_up, conv_w,
                conv_b, w_down, ln2_g, ln2_b):
    wi = w_in[l]
    d = wi.shape[0]
    o = [0, Q_LORA, Q_LORA + KV_LORA, Q_LORA + KV_LORA + QK_ROPE]
    o += [o[3] + H_B * D_B, o[3] + 2 * H_B * D_B, o[3] + 3 * H_B * D_B]
    cq, ckv, kpe = wi[:, :o[1]], wi[:, o[1]:o[2]], wi[:, o[2]:o[3]]
    qb, kb, vb, qm = wi[:, o[3]:o[4]], wi[:, o[4]:o[5]], wi[:, o[5]:o[6]], wi[:, o[6]:]
    k1, k2 = kpe[:, :ROPE_HALF], kpe[:, ROPE_HALF:]
    zpad = jnp.zeros((d, LANES - QK_ROPE), wi.dtype)
    win = jnp.concatenate([cq, ckv, qb, kb, vb, qm, k1, k2, zpad, k2, k1, zpad], axis=1).astype(BF16)
    uq = w_uq[l].reshape(Q_LORA, H_A, QK_NOPE + QK_ROPE)
    nope, p1, p2 = uq[..., :QK_NOPE], uq[..., QK_NOPE:QK_NOPE + ROPE_HALF], uq[..., QK_NOPE + ROPE_HALF:]
    z32 = jnp.zeros((Q_LORA, H_A, 64 - QK_ROPE), uq.dtype)
    z96 = jnp.zeros((Q_LORA, H_A, LANES - QK_ROPE), uq.dtype)
    wa = jnp.concatenate([p1, p2, z32, nope], axis=2).reshape(Q_LORA, H_A * LANES).astype(BF16)
    wb = jnp.concatenate([p2, p1, z96], axis=2).reshape(Q_LORA, H_A * LANES).astype(BF16)
    uk, uv = w_uk[l], w_uv[l]
    wk = jnp.concatenate([jnp.zeros_like(uk), uk], axis=2).reshape(KV_LORA, H_A * LANES).astype(BF16)
    wv = jnp.concatenate([uv, jnp.zeros_like(uv)], axis=2).reshape(KV_LORA, H_A * LANES).astype(BF16)
    uk_t = jnp.transpose(uk, (1, 2, 0))
    wql = jnp.concatenate([jnp.zeros_like(uk_t), uk_t], axis=1).astype(BF16)
    eye = jnp.eye(H_A, dtype=uv.dtype)
    wuv_bd = (eye[:, None, :, None] * jnp.transpose(uv, (1, 0, 2))[:, :, None, :]).reshape(H_A * KV_LORA, H_A * V_A).astype(BF16)
    up = w_up[l]
    return dict(
        win=win, qn=mla_q_norm[l][None], kvn=mla_kv_norm[l][None], wa=wa, wb=wb, wk=wk, wv=wv, wql=wql, wuv_bd=wuv_bd,
        wmem=w_mem_kv[l].astype(BF16), wo=w_out[l].astype(BF16), ln1_g=ln1_g[l][None], ln1_b=ln1_b[l][None],
        wg=up[:, :D_FF].astype(BF16), wv_up=up[:, D_FF:].astype(BF16), wd=w_down[l].astype(BF16),
        cw=conv_w[l], cb=conv_b[l][None], ln2_g=ln2_g[l][None], ln2_b=ln2_b[l][None])


def _rope_tables(pos):
    inv = ROPE_THETA ** (-jnp.arange(0, QK_ROPE, 2, dtype=F32) / QK_ROPE)
    ang = pos.astype(F32)[:, None] * inv
    cos, sin = jnp.cos(ang), jnp.sin(ang)
    n = pos.shape[0]
    ones = jnp.ones((n, LANES - QK_ROPE), F32)
    return (jnp.concatenate([cos, cos, ones], axis=1), jnp.concatenate([-sin, sin, 0.0 * ones], axis=1))


def kernel(x_prompt, x_sample, cache_mla_ckv, cache_mla_kpe, cache_moba_k, cache_moba_v, cache_mem_k, cache_mem_v, state_conv, page_table, mem_prompt, w_in, mla_q_norm, mla_kv_norm, w_uq, w_uk, w_uv, w_mem_kv, w_out, ln1_g, ln1_b, w_up, conv_w, conv_b, w_down, ln2_g, ln2_b):
    b, s, d = x_prompt.shape
    ns, t_s, _ = x_sample.shape
    depth = w_in.shape[0]
    mem_len = mem_prompt.shape[1]
    n_pages, page = page_table.shape[1], cache_mla_ckv.shape[2]
    past = n_pages * page
    assert s % MOBA_BLOCK == 0 and s // MOBA_BLOCK <= MAX_MOBA_BLOCKS and past % MOBA_BLOCK == 0
    assert t_s == SUBLANES and past // MOBA_BLOCK <= LANES
    alpha = (2 * depth) ** 0.25

    cache_kpt = jnp.transpose(cache_mla_kpe, (0, 1, 3, 2))
    cache_kt = jnp.transpose(cache_moba_k, (0, 1, 3, 4, 2))
    cache_vt = jnp.transpose(cache_moba_v, (0, 1, 3, 4, 2))
    mem_kt = jnp.transpose(cache_mem_k, (0, 1, 3, 4, 2))
    mem_vt = jnp.transpose(cache_mem_v, (0, 1, 3, 4, 2))

    cos_p, sin_p = _rope_tables(jnp.arange(s))
    tm_s = min(512, ns * t_s)
    cos_s, sin_s = _rope_tables(jnp.tile(past + jnp.arange(t_s), tm_s // t_s))
    blk_of = jnp.arange(s) // MOBA_BLOCK
    avg = jnp.where((jnp.arange(LANES)[:, None] - ONEHOT0) == blk_of[None, :], 1.0 / MOBA_BLOCK, 0.0).astype(BF16)

    hp = x_prompt.reshape(b * s, d)
    hs = x_sample.reshape(ns * t_s, d)
    mem2 = mem_prompt.reshape(b * mem_len, d)
    outs = {k: [] for k in ("p_ckv", "p_kpe", "p_k", "p_v", "p_mk", "p_mv", "p_cv", "s_ckv", "s_kpe", "s_k", "s_v", "s_cv")}
    for l in range(depth):
        w = _pack_layer(l, w_in, mla_q_norm, mla_kv_norm, w_uq, w_uk, w_uv, w_mem_kv, w_out, ln1_g, ln1_b, w_up,
                        conv_w, conv_b, w_down, ln2_g, ln2_b)
        ckv, kpe, kb, vb, qa, ka, va, qbp, kbp, vbb, qmp = _proj_prompt(hp, w, cos_p, sin_p, s)
        mk, mv, mkp, mvb = _memkv(mem2, w["wmem"])
        r3 = lambda a: a.reshape(b, -1, a.shape[-1])
        o_a = _flash(r3(qa), r3(ka), r3(va), "causal")
        o_b = _flash(r3(qbp), r3(kbp), r3(vbb), "moba", avg)
        o_m = _flash(r3(qmp), r3(mkp), r3(mvb), "full")
        f2 = lambda a: a.reshape(b * s, a.shape[-1])
        hp = _outproj(hp, f2(o_a), f2(o_b), f2(o_m), w, alpha)
        hp, cv = _ffn(hp, w, alpha, s)
        outs["p_ckv"].append(ckv.reshape(b, s, KV_LORA))
        outs["p_kpe"].append(kpe.reshape(b, s, QK_ROPE))
        outs["p_k"].append(kb.reshape(b, s, H_B, D_B))
        outs["p_v"].append(vb.reshape(b, s, H_B, D_B))
        outs["p_mk"].append(mk.reshape(b, mem_len, H_M, D_M))
        outs["p_mv"].append(mv.reshape(b, mem_len, H_M, D_M))
        outs["p_cv"].append(cv)
        ckv, kpe, kb, vb, qa, qlat, qb, qm = _proj_sample(hs, w, cos_s, sin_s)
        o_lat = _mla_sample(page_table, qlat.reshape(ns, t_s * H_A, KV_LORA), qa.reshape(ns, t_s * H_A, LANES),
                            ckv.reshape(ns, t_s, KV_LORA), kpe.reshape(ns, t_s, QK_ROPE), cache_mla_ckv, cache_kpt, l)
        t3 = lambda a: a.reshape(ns, t_s, a.shape[-1])
        o_b, o_m = _moba_sample(page_table, t3(qb), t3(kb), t3(vb), t3(qm), mem_kt, mem_vt, cache_kt, cache_vt, l)
        hs = _outproj(hs, o_lat.reshape(ns * t_s, H_A * KV_LORA), o_b.reshape(ns * t_s, -1), o_m.reshape(ns * t_s, -1),
                      w, alpha, wuv=w["wuv_bd"])
        prev = state_conv[l]
        zrow = jnp.zeros((ns, t_s - 1, D_FF), F32)
        s1 = jnp.concatenate([prev[:, 1:2], zrow], axis=1).reshape(ns * t_s, D_FF)
        s2 = jnp.concatenate([prev, zrow[:, 1:]], axis=1).reshape(ns * t_s, D_FF)
        hs, gate = _ffn(hs, w, alpha, t_s, prev=(s1, s2))
        outs["s_ckv"].append(ckv.reshape(ns, t_s, KV_LORA))
        outs["s_kpe"].append(kpe.reshape(ns, t_s, QK_ROPE))
        outs["s_k"].append(kb.reshape(ns, t_s, H_B, D_B))
        outs["s_v"].append(vb.reshape(ns, t_s, H_B, D_B))
        outs["s_cv"].append(gate.reshape(ns, t_s, D_FF)[:, t_s - (CONV_W - 1):])
    st = lambda k: jnp.stack(outs[k])
    return (hp.reshape(b, s, d), hs.reshape(ns, t_s, d),
            st("p_ckv"), st("p_kpe"), st("p_k"), st("p_v"), st("p_mk"), st("p_mv"), st("p_cv"),
            st("s_ckv"), st("s_kpe"), st("s_k"), st("s_v"), st("s_cv"))
```
